```python
import math
import jax, jax.numpy as jnp
from jax import lax
import numpy as np

D_MODEL = 2048
BATCH = 16
SEQ = 2048
DEPTH = 4
DEC_BATCH = 8
DEC_SEQ = 2048
PAST_LEN = 128

CONV_CH = 1024
N_HEADS = 8
HEAD_DIM = 128
ATTN_W = N_HEADS * HEAD_DIM
DIL_PAIRS = ((128, 1), (512, 4), (2048, 16))
ROPE_THETA = 10000.0
N_FOURIER_GROUPS = 4
D_FF = 5632
N_EVEN = (DEPTH + 1) // 2
N_ODD = DEPTH // 2
D_IN_MIX = 3 * CONV_CH + 3 * ATTN_W
ALPHA = (2 * DEPTH) ** 0.25
BETA = (8 * DEPTH) ** -0.25
LN_EPS = 1e-5

kernel_name = "hybrid_conv_dilattn_fnet_encoder"


def layer_norm(x, g, b):
    xf = x.astype(jnp.float32)
    mu = jnp.mean(xf, axis=-1, keepdims=True)
    var = jnp.mean(jnp.square(xf - mu), axis=-1, keepdims=True)
    y = (xf - mu) * lax.rsqrt(var + LN_EPS) * g.astype(jnp.float32) + b.astype(jnp.float32)
    return y.astype(x.dtype)


def dwconv3(x, w):
    xp = jnp.pad(x, ((0, 0), (1, 1), (0, 0)))
    return xp[:, :-2] * w[0] + xp[:, 1:-1] * w[1] + xp[:, 2:] * w[2]


def rope(x):
    S = x.shape[1]
    half = HEAD_DIM // 2
    inv = 1.0 / (ROPE_THETA ** (jnp.arange(half, dtype=jnp.float32) / half))
    ang = jnp.arange(S, dtype=jnp.float32)[:, None] * inv[None, :]
    cos = jnp.cos(ang)[None, :, None, :]
    sin = jnp.sin(ang)[None, :, None, :]
    xf = x.astype(jnp.float32)
    x1, x2 = xf[..., :half], xf[..., half:]
    return jnp.concatenate([x1 * cos - x2 * sin, x2 * cos + x1 * sin], axis=-1).astype(x.dtype)


def dilated_branch(q, k, v, window, dilation):
    Bsz, H, S, hd = q.shape
    r = dilation
    n_side = window // (2 * dilation)
    QB = n_side
    L = S // r
    nb = -(-L // QB)
    Lp = nb * QB

    def to_phase(t):
        return t.reshape(Bsz, H, L, r, hd).transpose(0, 1, 3, 2, 4)

    qp = jnp.pad(to_phase(q), ((0, 0), (0, 0), (0, 0), (0, Lp - L), (0, 0)))
    qb = qp.reshape(Bsz, H, r, nb, QB, hd)
    pad_k = ((0, 0), (0, 0), (0, 0), (QB, Lp - L + QB), (0, 0))
    kb = jnp.pad(to_phase(k), pad_k).reshape(Bsz, H, r, nb + 2, QB, hd)
    vb = jnp.pad(to_phase(v), pad_k).reshape(Bsz, H, r, nb + 2, QB, hd)
    kb = jnp.concatenate([kb[:, :, :, 0:nb], kb[:, :, :, 1:nb + 1], kb[:, :, :, 2:nb + 2]], axis=-2)
    vb = jnp.concatenate([vb[:, :, :, 0:nb], vb[:, :, :, 1:nb + 1], vb[:, :, :, 2:nb + 2]], axis=-2)

    a = jnp.arange(QB)[:, None]
    c = jnp.arange(3 * QB)[None, :]
    band = jnp.abs(c - QB - a) <= n_side
    kl = jnp.arange(nb)[:, None] * QB + jnp.arange(3 * QB)[None, :] - QB
    in_range = (kl >= 0) & (kl < L)
    mask = band[None, :, :] & in_range[:, None, :]

    s = jnp.einsum('bhrnqd,bhrnkd->bhrnqk', qb, kb)
    s = jnp.where(mask, s, -jnp.inf)
    m = jnp.max(s, axis=-1, keepdims=True)
    p = jnp.exp(s - m)
    den = jnp.sum(p, axis=-1, keepdims=True)
    o = jnp.einsum('bhrnqk,bhrnkd->bhrnqd', p, vb) / den
    lse = (m + jnp.log(den))[..., 0]

    o = o.reshape(Bsz, H, r, Lp, hd)[:, :, :, :L].transpose(0, 1, 3, 2, 4).reshape(Bsz, H, S, hd)
    lse = lse.reshape(Bsz, H, r, Lp)[..., :L].transpose(0, 1, 3, 2).reshape(Bsz, H, S)
    return o, lse


def dilated_attention(q, k, v):
    outs, lses = [], []
    for window, dilation in DIL_PAIRS:
        o, l = dilated_branch(q, k, v, window, dilation)
        outs.append(o)
        lses.append(l)
    w = jax.nn.softmax(jnp.stack(lses, axis=0), axis=0)
    return jnp.sum(w[..., None] * jnp.stack(outs, axis=0), axis=0)


def even_mixer(x, w_in, conv_w, w_out):
    Bsz, S, _ = x.shape
    h = x @ w_in
    c = CONV_CH
    bg, cg, xv, q, k, v = jnp.split(
        h, [c, 2 * c, 3 * c, 3 * c + ATTN_W, 3 * c + 2 * ATTN_W], axis=-1)
    y_conv = bg * dwconv3(cg * xv, conv_w)
    q = rope(q.reshape(Bsz, S, N_HEADS, HEAD_DIM))
    k = rope(k.reshape(Bsz, S, N_HEADS, HEAD_DIM))
    v = v.reshape(Bsz, S, N_HEADS, HEAD_DIM)
    qf = q.transpose(0, 2, 1, 3).astype(jnp.float32) * (HEAD_DIM ** -0.5)
    kf = k.transpose(0, 2, 1, 3).astype(jnp.float32)
    vf = v.transpose(0, 2, 1, 3).astype(jnp.float32)
    o = dilated_attention(qf, kf, vf)
    y_attn = o.transpose(0, 2, 1, 3).reshape(Bsz, S, ATTN_W).astype(x.dtype)
    return jnp.concatenate([y_conv, y_attn], axis=-1) @ w_out


def fourier_mixer(x, w_out):
    Bsz, S, D = x.shape
    xg = x.astype(jnp.float32).reshape(Bsz, S, N_FOURIER_GROUPS, D // N_FOURIER_GROUPS)
    f = jnp.fft.fft2(xg, axes=(1, 3), norm="ortho").real
    return f.reshape(Bsz, S, D).astype(x.dtype) @ w_out


def conv_ffn(x, w_up, conv_w, w_down):
    h = dwconv3(x @ w_up, conv_w)
    g, u = jnp.split(h, 2, axis=-1)
    return (jax.nn.silu(g) * u) @ w_down


def trunk(x, w_in_mix, conv_short, w_out_mix, w_out_fourier, ln_mix_g, ln_mix_b,
          w_up, conv_ffn_w, w_down, ln_ffn_g, ln_ffn_b):
    for l in range(DEPTH):
        if l % 2 == 0:
            i = l // 2
            mix = even_mixer(x, w_in_mix[i], conv_short[i], w_out_mix[i])
        else:
            mix = fourier_mixer(x, w_out_fourier[l // 2])
        x = layer_norm(ALPHA * x + mix, ln_mix_g[l], ln_mix_b[l])
        x = layer_norm(ALPHA * x + conv_ffn(x, w_up[l], conv_ffn_w[l], w_down[l]), ln_ffn_g[l], ln_ffn_b[l])
    return x


def setup_inputs(seed: int = 0) -> dict:
    key = jax.random.key(seed)
    ks = jax.random.split(key, 13)
    f32 = jnp.float32
    D = D_MODEL
    nrm = lambda k, shape, scale: jax.random.normal(k, shape, f32) * scale
    return {
        "x_prompt": jax.random.normal(ks[0], (BATCH, SEQ, D), f32),
        "x_sample": jax.random.normal(ks[1], (DEC_BATCH, DEC_SEQ, D), f32),
        "w_in_mix": nrm(ks[2], (N_EVEN, D, D_IN_MIX), D ** -0.5),
        "conv_short": nrm(ks[3], (N_EVEN, 3, CONV_CH), 3 ** -0.5),
        "w_out_mix": nrm(ks[4], (N_EVEN, CONV_CH + ATTN_W, D), BETA * (CONV_CH + ATTN_W) ** -0.5),
        "w_out_fourier": nrm(ks[5], (N_ODD, D, D), BETA * D ** -0.5),
        "ln_mix_g": 1.0 + nrm(ks[6], (DEPTH, D), 0.02),
        "ln_mix_b": nrm(ks[7], (DEPTH, D), 0.02),
        "w_up": nrm(ks[8], (DEPTH, D, 2 * D_FF), D ** -0.5),
        "conv_ffn_w": nrm(ks[9], (DEPTH, 3, 2 * D_FF), 3 ** -0.5),
        "w_down": nrm(ks[10], (DEPTH, D_FF, D), BETA * D_FF ** -0.5),
        "ln_ffn_g": 1.0 + nrm(ks[11], (DEPTH, D), 0.02),
        "ln_ffn_b": nrm(ks[12], (DEPTH, D), 0.02),
    }


def reference(x_prompt, x_sample, w_in_mix, conv_short, w_out_mix, w_out_fourier, ln_mix_g, ln_mix_b,
              w_up, conv_ffn_w, w_down, ln_ffn_g, ln_ffn_b):
    y_prompt = trunk(x_prompt, w_in_mix, conv_short, w_out_mix, w_out_fourier, ln_mix_g, ln_mix_b,
                     w_up, conv_ffn_w, w_down, ln_ffn_g, ln_ffn_b)
    y_sample = trunk(x_sample, w_in_mix, conv_short, w_out_mix, w_out_fourier, ln_mix_g, ln_mix_b,
                     w_up, conv_ffn_w, w_down, ln_ffn_g, ln_ffn_b)
    return (y_prompt, y_sample)
```

```python
import functools

import numpy as np
import jax
import jax.numpy as jnp
from jax import lax
from jax.experimental import pallas as pl
from jax.experimental.pallas import tpu as pltpu

F32 = jnp.float32
BF16 = jnp.bfloat16

D_MODEL = 2048
SEQ = 2048
DEPTH = 4
CONV_CH = 1024
N_HEADS = 8
HEAD_DIM = 128
ATTN_W = N_HEADS * HEAD_DIM
DIL_PAIRS = ((128, 1), (512, 4), (2048, 16))
ROPE_THETA = 10000.0
N_FOURIER_GROUPS = 4
FOURIER_CH = D_MODEL // N_FOURIER_GROUPS
D_FF = 5632
ALPHA = (2 * DEPTH) ** 0.25
LN_EPS = 1e-5

MIB = 1024 * 1024
NEG_BIG = -1e30

MM_TM = 1024
MM_TN = 1024
LN_TM = 512
ATT_QB = 256
GATE_TC = 256
FFN_TF = 512
FFN_RB = 512
FFN_RE = 256
FFN_PAD = 8


def _params(sem, vmem_mib):
    return pltpu.CompilerParams(dimension_semantics=sem, vmem_limit_bytes=vmem_mib * MIB)


def _const_spec(shape):
    nd = len(shape)
    return pl.BlockSpec(shape, lambda *_: (0,) * nd, pipeline_mode=pl.Buffered(1))


def _layer_norm(z, g, b):
    mu = jnp.mean(z, axis=-1, keepdims=True)
    zc = z - mu
    var = jnp.mean(zc * zc, axis=-1, keepdims=True)
    return zc * lax.rsqrt(var + LN_EPS) * g + b


def _mm_kernel(x_ref, w_ref, o_ref):
    o_ref[...] = jnp.dot(x_ref[...], w_ref[...], preferred_element_type=F32).astype(o_ref.dtype)


def _project(xb, w, col0, ncols, out_dtype):
    m, k = xb.shape
    c0 = col0 // MM_TN
    return pl.pallas_call(
        _mm_kernel,
        grid=(m // MM_TM, ncols // MM_TN),
        in_specs=[pl.BlockSpec((MM_TM, k), lambda i, j: (i, 0)),
                  pl.BlockSpec((k, MM_TN), lambda i, j: (0, c0 + j))],
        out_specs=pl.BlockSpec((MM_TM, MM_TN), lambda i, j: (i, j)),
        out_shape=jax.ShapeDtypeStruct((m, ncols), out_dtype),
        compiler_params=_params(("parallel", "arbitrary"), 40),
        name="project",
    )(xb, w)


def _qkv_kernel(x_ref, w_ref, cos_ref, sin_ref, o_ref):
    j = pl.program_id(1)
    acc = jnp.dot(x_ref[...], w_ref[...], preferred_element_type=F32)

    @pl.when(j == 2)
    def _():
        o_ref[...] = acc.astype(BF16)

    @pl.when(j < 2)
    def _():
        scale = jnp.where(j == 0, HEAD_DIM ** -0.5, 1.0).astype(F32)
        cos = cos_ref[...]
        sin = sin_ref[...]
        for h in range(N_HEADS):
            xh = acc[:, h * HEAD_DIM:(h + 1) * HEAD_DIM]
            rot = pltpu.roll(xh, HEAD_DIM // 2, axis=1)
            o_ref[:, h * HEAD_DIM:(h + 1) * HEAD_DIM] = ((xh * cos + rot * sin) * scale).astype(BF16)


def _project_qkv(xb, w_in, cos_full, sin_signed):
    m, k = xb.shape
    c0 = 3 * CONV_CH // ATTN_W
    n_pos_tiles = SEQ // MM_TM
    return pl.pallas_call(
        _qkv_kernel,
        grid=(m // MM_TM, 3),
        in_specs=[pl.BlockSpec((MM_TM, k), lambda i, j: (i, 0)),
                  pl.BlockSpec((k, ATTN_W), lambda i, j: (0, c0 + j)),
                  pl.BlockSpec((MM_TM, HEAD_DIM), lambda i, j: (i % n_pos_tiles, 0)),
                  pl.BlockSpec((MM_TM, HEAD_DIM), lambda i, j: (i % n_pos_tiles, 0))],
        out_specs=pl.BlockSpec((MM_TM, ATTN_W), lambda i, j: (i, j)),
        out_shape=jax.ShapeDtypeStruct((m, 3 * ATTN_W), BF16),
        compiler_params=_params(("parallel", "arbitrary"), 40),
        name="project_qkv",
    )(xb, w_in, cos_full, sin_signed)


def _shift_rows(t, row):
    n = t.shape[0]
    prev = jnp.where(row == 0, 0.0, pltpu.roll(t, 1, axis=0))
    nxt = jnp.where(row == n - 1, 0.0, pltpu.roll(t, n - 1, axis=0))
    return prev, nxt


def _gate_kernel(bg_ref, cg_ref, xv_ref, w_ref, o_ref):
    t = cg_ref[0] * xv_ref[0]
    row = lax.broadcasted_iota(jnp.int32, t.shape, 0)
    prev, nxt = _shift_rows(t, row)
    w = w_ref[...]
    y = prev * w[0:1] + t * w[1:2] + nxt * w[2:3]
    o_ref[0] = (bg_ref[0] * y).astype(BF16)


def _gated_conv(hc, conv_w):
    b = hc.shape[0]
    nb = CONV_CH // GATE_TC
    blk = (1, SEQ, GATE_TC)
    return pl.pallas_call(
        _gate_kernel,
        grid=(b, nb),
        in_specs=[pl.BlockSpec(blk, lambda i, j: (i, 0, j)),
                  pl.BlockSpec(blk, lambda i, j: (i, 0, nb + j)),
                  pl.BlockSpec(blk, lambda i, j: (i, 0, 2 * nb + j)),
                  pl.BlockSpec((3, GATE_TC), lambda i, j: (0, j))],
        out_specs=pl.BlockSpec(blk, lambda i, j: (i, 0, j)),
        out_shape=jax.ShapeDtypeStruct((b, SEQ, CONV_CH), BF16),
        compiler_params=_params(("parallel", "arbitrary"), 40),
        name="gated_conv",
    )(hc, hc, hc, conv_w)


def _attention_bias_table():
    nb = SEQ // ATT_QB
    a = np.arange(ATT_QB)[:, None]
    c = np.arange((2 * nb - 1) * ATT_QB)[None, :]
    d = c - (nb - 1) * ATT_QB - a
    count = np.zeros(d.shape, np.int64)
    for window, dilation in DIL_PAIRS:
        count += ((d % dilation == 0) & (np.abs(d) <= window // 2)).astype(np.int64)
    bias = np.where(count > 0, np.log(np.maximum(count, 1).astype(np.float64)), NEG_BIG)
    return bias.astype(np.float32)


def _attention_key_blocks(qi):
    nb = SEQ // ATT_QB
    reach = max(w // 2 for w, _ in DIL_PAIRS)
    span = -(-reach // ATT_QB)
    return max(0, qi - span), min(nb, qi + span + 1)


def _attn_kernel(q_ref, k_ref, v_ref, bias_ref, o_ref):
    nb = SEQ // ATT_QB
    for qi in range(nb):
        k0, k1 = _attention_key_blocks(qi)
        rows = slice(qi * ATT_QB, (qi + 1) * ATT_QB)
        keys = slice(k0 * ATT_QB, k1 * ATT_QB)
        q = q_ref[0, rows, :]
        s = lax.dot_general(q, k_ref[0, keys, :], (((1,), (1,)), ((), ())),
                            preferred_element_type=F32)
        b0 = (nb - 1 - qi + k0) * ATT_QB
        s = s + bias_ref[:, b0:b0 + (k1 - k0) * ATT_QB]
        m = jnp.max(s, axis=-1, keepdims=True)
        p = jnp.exp(s - m)
        den = jnp.sum(p, axis=-1, keepdims=True)
        o = jnp.dot(p.astype(BF16), v_ref[0, keys, :], preferred_element_type=F32)
        o_ref[0, rows, :] = (o / den).astype(BF16)


def _attention(qkv, bias):
    b = qkv.shape[0]
    blk = (1, SEQ, HEAD_DIM)
    return pl.pallas_call(
        _attn_kernel,
        grid=(b, N_HEADS),
        in_specs=[pl.BlockSpec(blk, lambda i, h: (i, 0, h)),
                  pl.BlockSpec(blk, lambda i, h: (i, 0, N_HEADS + h)),
                  pl.BlockSpec(blk, lambda i, h: (i, 0, 2 * N_HEADS + h)),
                  _const_spec(bias.shape)],
        out_specs=pl.BlockSpec(blk, lambda i, h: (i, 0, h)),
        out_shape=jax.ShapeDtypeStruct((b, SEQ, ATTN_W), BF16),
        compiler_params=_params(("parallel", "arbitrary"), 40),
        name="dilated_attention",
    )(qkv, qkv, qkv, bias)


def _dft_matrices(n, scale):
    idx = jnp.arange(n, dtype=jnp.int32)
    m = (idx[:, None] * idx[None, :]) % n
    ang = m.astype(F32) * (2.0 * np.pi / n)
    return jnp.cos(ang) * scale, jnp.sin(ang) * scale


def _dft_channel_kernel(x_ref, w_ref, a_ref, b_ref):
    r = jnp.dot(x_ref[...], w_ref[...], preferred_element_type=F32)
    a_ref[...] = r[:, :FOURIER_CH].astype(BF16)
    b_ref[...] = r[:, FOURIER_CH:].astype(BF16)


def _dft_channels(xb, w_cs):
    m, d = xb.shape
    blk = pl.BlockSpec((MM_TM, FOURIER_CH), lambda i, g: (i, g))
    return pl.pallas_call(
        _dft_channel_kernel,
        grid=(m // MM_TM, N_FOURIER_GROUPS),
        in_specs=[blk, _const_spec(w_cs.shape)],
        out_specs=[blk, blk],
        out_shape=[jax.ShapeDtypeStruct((m, d), BF16)] * 2,
        compiler_params=_params(("parallel", "arbitrary"), 40),
        name="dft_channels",
    )(xb, w_cs)


def _dft_seq_kernel(c_ref, s_ref, a_ref, b_ref, y_ref):
    y = jnp.dot(c_ref[...], a_ref[0], preferred_element_type=F32)
    y = y + jnp.dot(s_ref[...], b_ref[0], preferred_element_type=F32)
    y_ref[0] = y.astype(BF16)


def _dft_sequence(cos_s, neg_sin_s, a, b):
    bsz = a.shape[0]
    tn = FOURIER_CH
    blk = pl.BlockSpec((1, SEQ, tn), lambda i, j: (i, 0, j))
    return pl.pallas_call(
        _dft_seq_kernel,
        grid=(bsz, D_MODEL // tn),
        in_specs=[_const_spec(cos_s.shape), _const_spec(neg_sin_s.shape), blk, blk],
        out_specs=blk,
        out_shape=jax.ShapeDtypeStruct(a.shape, BF16),
        compiler_params=_params(("parallel", "arbitrary"), 48),
        name="dft_sequence",
    )(cos_s, neg_sin_s, a, b)


def _proj_ln_kernel(*refs, n_lhs):
    lhs = refs[:n_lhs]
    ws = refs[n_lhs:2 * n_lhs]
    res_ref, g_ref, b_ref, y_ref, yb_ref = refs[2 * n_lhs:]
    acc = jnp.dot(lhs[0][...], ws[0][...], preferred_element_type=F32)
    for l, w in zip(lhs[1:], ws[1:]):
        acc = acc + jnp.dot(l[...], w[...], preferred_element_type=F32)
    y = _layer_norm(ALPHA * res_ref[...] + acc, g_ref[...], b_ref[...])
    y_ref[...] = y
    yb_ref[...] = y.astype(BF16)


def _project_residual_ln(lhs_list, w_list, res, g, b):
    m, d = res.shape
    n = len(lhs_list)
    in_specs = [pl.BlockSpec((LN_TM, l.shape[1]), lambda i: (i, 0)) for l in lhs_list]
    in_specs += [_const_spec(w.shape) for w in w_list]
    in_specs += [pl.BlockSpec((LN_TM, d), lambda i: (i, 0)), _const_spec((1, d)), _const_spec((1, d))]
    out_blk = pl.BlockSpec((LN_TM, d), lambda i: (i, 0))
    return pl.pallas_call(
        functools.partial(_proj_ln_kernel, n_lhs=n),
        grid=(m // LN_TM,),
        in_specs=in_specs,
        out_specs=[out_blk, out_blk],
        out_shape=[jax.ShapeDtypeStruct((m, d), F32), jax.ShapeDtypeStruct((m, d), BF16)],
        compiler_params=_params(("parallel",), 48),
        name="project_residual_ln",
    )(*lhs_list, *w_list, res, g.reshape(1, d), b.reshape(1, d))


def _ffn_kernel(xb_ref, wg_ref, wu_ref, cg_ref, cu_ref, wd_ref, res_ref, g_ref, b_ref,
                y_ref, yb_ref, acc_ref, hg_ref, hu_ref, *, nf):
    i = pl.program_id(0)
    j = pl.program_id(1)
    n_rb = SEQ // FFN_RB

    @pl.when(jnp.logical_and(i == 0, j == 0))
    def _():
        acc_ref[...] = jnp.zeros_like(acc_ref)
        zpad = jnp.zeros((FFN_PAD, FFN_TF), F32)
        for h_ref in (hg_ref, hu_ref):
            h_ref[0:FFN_PAD, :] = zpad
            h_ref[FFN_PAD + SEQ:, :] = zpad

    @pl.when(j < nf)
    def _():
        for rb in range(n_rb):
            r0 = rb * FFN_RB
            xr = xb_ref[0, r0:r0 + FFN_RB, :]
            hg_ref[FFN_PAD + r0:FFN_PAD + r0 + FFN_RB, :] = jnp.dot(
                xr, wg_ref[...], preferred_element_type=F32)
            hu_ref[FFN_PAD + r0:FFN_PAD + r0 + FFN_RB, :] = jnp.dot(
                xr, wu_ref[...], preferred_element_type=F32)
        cg = cg_ref[...]
        cu = cu_ref[...]
        for rb in range(n_rb):
            r0 = rb * FFN_RB

            def conv(h_ref, c):
                prev = h_ref[FFN_PAD + r0 - 1:FFN_PAD + r0 - 1 + FFN_RB, :]
                cur = h_ref[FFN_PAD + r0:FFN_PAD + r0 + FFN_RB, :]
                nxt = h_ref[FFN_PAD + r0 + 1:FFN_PAD + r0 + 1 + FFN_RB, :]
                return prev * c[0:1] + cur * c[1:2] + nxt * c[2:3]

            gate = conv(hg_ref, cg)
            up = conv(hu_ref, cu)
            act = (gate * jax.nn.sigmoid(gate) * up).astype(BF16)
            acc_ref[r0:r0 + FFN_RB, :] += jnp.dot(act, wd_ref[...], preferred_element_type=F32)

    @pl.when(j >= nf)
    def _():
        e0 = pl.multiple_of((j - nf) * FFN_RE, FFN_RE)
        rows = pl.ds(e0, FFN_RE)
        y = _layer_norm(ALPHA * res_ref[0] + acc_ref[rows, :], g_ref[...], b_ref[...])
        y_ref[0] = y
        yb_ref[0] = y.astype(BF16)
        acc_ref[rows, :] = jnp.zeros((FFN_RE, D_MODEL), F32)


def _ffn(xb, res, w_up, conv_w, w_down, g, b):
    bsz = xb.shape[0]
    d = D_MODEL
    nf = D_FF // FFN_TF
    ne = SEQ // FFN_RE

    def chunk(j):
        return jnp.where(j < nf, j, 0)

    def erow(j):
        return jnp.maximum(j - nf, 0)

    return pl.pallas_call(
        functools.partial(_ffn_kernel, nf=nf),
        grid=(bsz, nf + ne),
        in_specs=[
            pl.BlockSpec((1, SEQ, d), lambda i, j: (i, 0, 0), pipeline_mode=pl.Buffered(1)),
            pl.BlockSpec((d, FFN_TF), lambda i, j: (0, chunk(j))),
            pl.BlockSpec((d, FFN_TF), lambda i, j: (0, nf + chunk(j))),
            pl.BlockSpec((3, FFN_TF), lambda i, j: (0, chunk(j))),
            pl.BlockSpec((3, FFN_TF), lambda i, j: (0, nf + chunk(j))),
            pl.BlockSpec((FFN_TF, d), lambda i, j: (chunk(j), 0)),
            pl.BlockSpec((1, FFN_RE, d), lambda i, j: (i, erow(j), 0)),
            _const_spec((1, d)),
            _const_spec((1, d)),
        ],
        out_specs=[pl.BlockSpec((1, FFN_RE, d), lambda i, j: (i, erow(j), 0)),
                   pl.BlockSpec((1, FFN_RE, d), lambda i, j: (i, erow(j), 0))],
        out_shape=[jax.ShapeDtypeStruct((bsz, SEQ, d), F32), jax.ShapeDtypeStruct((bsz, SEQ, d), BF16)],
        scratch_shapes=[pltpu.VMEM((SEQ, d), F32),
                        pltpu.VMEM((SEQ + 2 * FFN_PAD, FFN_TF), F32),
                        pltpu.VMEM((SEQ + 2 * FFN_PAD, FFN_TF), F32)],
        compiler_params=_params(("arbitrary", "arbitrary"), 58),
        name="conv_ffn_ln",
    )(xb, w_up, w_up, conv_w, conv_w, w_down, res, g.reshape(1, d), b.reshape(1, d))


def _rope_tables():
    half = HEAD_DIM // 2
    inv = 1.0 / (ROPE_THETA ** (jnp.arange(half, dtype=F32) / half))
    ang = jnp.arange(SEQ, dtype=F32)[:, None] * inv[None, :]
    cos = jnp.cos(ang)
    sin = jnp.sin(ang)
    return jnp.concatenate([cos, cos], axis=-1), jnp.concatenate([-sin, sin], axis=-1)


def _trunk(x, consts, weights):
    bsz = x.shape[0]
    m = bsz * SEQ
    d = D_MODEL
    xb = x.astype(BF16)
    for l in range(DEPTH):
        i = l // 2
        x2 = x.reshape(m, d)
        xb2 = xb.reshape(m, d)
        if l % 2 == 0:
            w_in = weights["w_in_mix"][i]
            hc = _project(xb2, w_in, 0, 3 * CONV_CH, F32).reshape(bsz, SEQ, 3 * CONV_CH)
            qkv = _project_qkv(xb2, w_in, consts["rope_cos"], consts["rope_sin"])
            y_conv = _gated_conv(hc, weights["conv_short"][i])
            y_attn = _attention(qkv.reshape(bsz, SEQ, 3 * ATTN_W), consts["attn_bias"])
            w_out = weights["w_out_mix"][i]
            lhs = [y_conv.reshape(m, CONV_CH), y_attn.reshape(m, ATTN_W)]
            ws = [w_out[:CONV_CH], w_out[CONV_CH:]]
        else:
            a, b = _dft_channels(xb2, consts["dft_ch"])
            y = _dft_sequence(consts["dft_cos"], consts["dft_neg_sin"],
                              a.reshape(bsz, SEQ, d), b.reshape(bsz, SEQ, d))
            lhs = [y.reshape(m, d)]
            ws = [weights["w_out_fourier"][i]]
        x2, xb2 = _project_residual_ln(lhs, ws, x2, weights["ln_mix_g"][l], weights["ln_mix_b"][l])
        x, xb = _ffn(xb2.reshape(bsz, SEQ, d), x2.reshape(bsz, SEQ, d), weights["w_up"][l],
                     weights["conv_ffn_w"][l], weights["w_down"][l],
                     weights["ln_ffn_g"][l], weights["ln_ffn_b"][l])
    return x


def kernel(x_prompt, x_sample, w_in_mix, conv_short, w_out_mix, w_out_fourier, ln_mix_g, ln_mix_b,
           w_up, conv_ffn_w, w_down, ln_ffn_g, ln_ffn_b):
    weights = {
        "w_in_mix": w_in_mix.astype(BF16), "conv_short": conv_short,
        "w_out_mix": w_out_mix.astype(BF16), "w_out_fourier": w_out_fourier.astype(BF16),
        "ln_mix_g": ln_mix_g, "ln_mix_b": ln_mix_b,
        "w_up": w_up.astype(BF16), "conv_ffn_w": conv_ffn_w, "w_down": w_down.astype(BF16),
        "ln_ffn_g": ln_ffn_g, "ln_ffn_b": ln_ffn_b,
    }
    cos_c, sin_c = _dft_matrices(FOURIER_CH, 2.0 ** -5)
    cos_s, sin_s = _dft_matrices(SEQ, 2.0 ** -5)
    rope_cos, rope_sin = _rope_tables()
    consts = {
        "rope_cos": rope_cos, "rope_sin": rope_sin,
        "attn_bias": jnp.asarray(_attention_bias_table()),
        "dft_ch": jnp.concatenate([cos_c, sin_c], axis=1).astype(BF16),
        "dft_cos": cos_s.astype(BF16), "dft_neg_sin": (-sin_s).astype(BF16),
    }
    return (_trunk(x_prompt, consts, weights), _trunk(x_sample, consts, weights))
```

```python
import functools

import numpy as np
import jax
import jax.numpy as jnp
from jax import lax
from jax.experimental import pallas as pl
from jax.experimental.pallas import tpu as pltpu

F32 = jnp.float32
BF16 = jnp.bfloat16

D_MODEL = 2048
SEQ = 2048
DEPTH = 4
CONV_CH = 1024
N_HEADS = 8
HEAD_DIM = 128
ATTN_W = N_HEADS * HEAD_DIM
DIL_PAIRS = ((128, 1), (512, 4), (2048, 16))
ROPE_THETA = 10000.0
N_FOURIER_GROUPS = 4
FOURIER_CH = D_MODEL // N_FOURIER_GROUPS
D_FF = 5632
ALPHA = (2 * DEPTH) ** 0.25
LN_EPS = 1e-5

MIB = 1024 * 1024
NEG_BIG = -1e30

MM_TM = 1024
SUB_TM = 256
LN_TM = 512
ATT_QB = 256
GATE_TC = 256
FFN_TF = 512
FFN_RB = 512
FFN_RC = 256
FFN_PAD = 8


def _params(sem, vmem_mib):
    return pltpu.CompilerParams(dimension_semantics=sem, vmem_limit_bytes=vmem_mib * MIB)


def _const_spec(shape):
    nd = len(shape)
    return pl.BlockSpec(shape, lambda *_: (0,) * nd, pipeline_mode=pl.Buffered(1))


def _layer_norm(z, g, b):
    mu = jnp.mean(z, axis=-1, keepdims=True)
    zc = z - mu
    var = jnp.mean(zc * zc, axis=-1, keepdims=True)
    return zc * lax.rsqrt(var + LN_EPS) * g + b


def _shift_rows(t):
    n = t.shape[0]
    row = lax.broadcasted_iota(jnp.int32, t.shape, 0)
    prev = jnp.where(row == 0, 0.0, pltpu.roll(t, 1, axis=0))
    nxt = jnp.where(row == n - 1, 0.0, pltpu.roll(t, n - 1, axis=0))
    return prev, nxt


def _conv_branch_kernel(x_ref, wb_ref, wc_ref, wx_ref, cw_ref, o_ref):
    x = x_ref[0]
    cg = jnp.dot(x, wc_ref[...], preferred_element_type=F32)
    xv = jnp.dot(x, wx_ref[...], preferred_element_type=F32)
    t = cg * xv
    prev, nxt = _shift_rows(t)
    w = cw_ref[...]
    y = prev * w[0:1] + t * w[1:2] + nxt * w[2:3]
    bg = jnp.dot(x, wb_ref[...], preferred_element_type=F32)
    o_ref[0] = (bg * y).astype(BF16)


def _conv_branch(xb, w_in, conv_w):
    bsz = xb.shape[0]
    nb = CONV_CH // GATE_TC
    return pl.pallas_call(
        _conv_branch_kernel,
        grid=(bsz, nb),
        in_specs=[pl.BlockSpec((1, SEQ, D_MODEL), lambda i, j: (i, 0, 0)),
                  pl.BlockSpec((D_MODEL, GATE_TC), lambda i, j: (0, j)),
                  pl.BlockSpec((D_MODEL, GATE_TC), lambda i, j: (0, nb + j)),
                  pl.BlockSpec((D_MODEL, GATE_TC), lambda i, j: (0, 2 * nb + j)),
                  pl.BlockSpec((3, GATE_TC), lambda i, j: (0, j))],
        out_specs=pl.BlockSpec((1, SEQ, GATE_TC), lambda i, j: (i, 0, j)),
        out_shape=jax.ShapeDtypeStruct((bsz, SEQ, CONV_CH), BF16),
        compiler_params=_params(("parallel", "arbitrary"), 48),
        name="conv_branch",
    )(xb, w_in, w_in, w_in, conv_w)


def _qkv_kernel(x_ref, w_ref, cos_ref, sin_ref, o_ref):
    j = pl.program_id(1)

    @pl.when(j == 2)
    def _():
        o_ref[...] = jnp.dot(x_ref[...], w_ref[...], preferred_element_type=F32).astype(BF16)

    @pl.when(j < 2)
    def _():
        scale = jnp.where(j == 0, HEAD_DIM ** -0.5, 1.0).astype(F32)
        for s in range(MM_TM // SUB_TM):
            rows = slice(s * SUB_TM, (s + 1) * SUB_TM)
            acc = jnp.dot(x_ref[rows, :], w_ref[...], preferred_element_type=F32)
            cos = cos_ref[rows, :]
            sin = sin_ref[rows, :]
            for h in range(N_HEADS):
                cols = slice(h * HEAD_DIM, (h + 1) * HEAD_DIM)
                xh = acc[:, cols]
                rot = pltpu.roll(xh, HEAD_DIM // 2, axis=1)
                o_ref[rows, cols] = ((xh * cos + rot * sin) * scale).astype(BF16)


def _project_qkv(xb, w_in, cos_full, sin_signed):
    m, k = xb.shape
    c0 = 3 * CONV_CH // ATTN_W
    n_pos_tiles = SEQ // MM_TM
    return pl.pallas_call(
        _qkv_kernel,
        grid=(m // MM_TM, 3),
        in_specs=[pl.BlockSpec((MM_TM, k), lambda i, j: (i, 0)),
                  pl.BlockSpec((k, ATTN_W), lambda i, j: (0, c0 + j)),
                  pl.BlockSpec((MM_TM, HEAD_DIM), lambda i, j: (i % n_pos_tiles, 0)),
                  pl.BlockSpec((MM_TM, HEAD_DIM), lambda i, j: (i % n_pos_tiles, 0))],
        out_specs=pl.BlockSpec((MM_TM, ATTN_W), lambda i, j: (i, j)),
        out_shape=jax.ShapeDtypeStruct((m, 3 * ATTN_W), BF16),
        compiler_params=_params(("parallel", "arbitrary"), 40),
        name="project_qkv",
    )(xb, w_in, cos_full, sin_signed)


def _attention_bias_table():
    nb = SEQ // ATT_QB
    a = np.arange(ATT_QB)[:, None]
    c = np.arange((2 * nb - 1) * ATT_QB)[None, :]
    d = c - (nb - 1) * ATT_QB - a
    count = np.zeros(d.shape, np.int64)
    for window, dilation in DIL_PAIRS:
        count += ((d % dilation == 0) & (np.abs(d) <= window // 2)).astype(np.int64)
    bias = np.where(count > 0, np.log(np.maximum(count, 1).astype(np.float64)), NEG_BIG)
    return bias.astype(np.float32)


def _attention_key_blocks(qi):
    nb = SEQ // ATT_QB
    reach = max(w // 2 for w, _ in DIL_PAIRS)
    span = -(-reach // ATT_QB)
    return max(0, qi - span), min(nb, qi + span + 1)


def _attn_kernel(q_ref, k_ref, v_ref, bias_ref, o_ref):
    nb = SEQ // ATT_QB
    for qi in range(nb):
        k0, k1 = _attention_key_blocks(qi)
        rows = slice(qi * ATT_QB, (qi + 1) * ATT_QB)
        keys = slice(k0 * ATT_QB, k1 * ATT_QB)
        q = q_ref[0, rows, :]
        s = lax.dot_general(q, k_ref[0, keys, :], (((1,), (1,)), ((), ())),
                            preferred_element_type=F32)
        b0 = (nb - 1 - qi + k0) * ATT_QB
        s = s + bias_ref[:, b0:b0 + (k1 - k0) * ATT_QB]
        m = jnp.max(s, axis=-1, keepdims=True)
        p = jnp.exp(s - m)
        den = jnp.sum(p, axis=-1, keepdims=True)
        o = jnp.dot(p.astype(BF16), v_ref[0, keys, :], preferred_element_type=F32)
        o_ref[0, rows, :] = (o / den).astype(BF16)


def _attention(qkv, bias):
    b = qkv.shape[0]
    blk = (1, SEQ, HEAD_DIM)
    return pl.pallas_call(
        _attn_kernel,
        grid=(b, N_HEADS),
        in_specs=[pl.BlockSpec(blk, lambda i, h: (i, 0, h)),
                  pl.BlockSpec(blk, lambda i, h: (i, 0, N_HEADS + h)),
                  pl.BlockSpec(blk, lambda i, h: (i, 0, 2 * N_HEADS + h)),
                  _const_spec(bias.shape)],
        out_specs=pl.BlockSpec(blk, lambda i, h: (i, 0, h)),
        out_shape=jax.ShapeDtypeStruct((b, SEQ, ATTN_W), BF16),
        compiler_params=_params(("parallel", "arbitrary"), 40),
        name="dilated_attention",
    )(qkv, qkv, qkv, bias)


def _dft_matrices(n, scale):
    idx = jnp.arange(n, dtype=jnp.int32)
    m = (idx[:, None] * idx[None, :]) % n
    ang = m.astype(F32) * (2.0 * np.pi / n)
    return jnp.cos(ang) * scale, jnp.sin(ang) * scale


def _dft_channel_kernel(x_ref, w_ref, a_ref, b_ref):
    r = jnp.dot(x_ref[...], w_ref[...], preferred_element_type=F32)
    a_ref[...] = r[:, :FOURIER_CH].astype(BF16)
    b_ref[...] = r[:, FOURIER_CH:].astype(BF16)


def _dft_channels(xb, w_cs):
    m, d = xb.shape
    blk = pl.BlockSpec((MM_TM, FOURIER_CH), lambda i, g: (i, g))
    return pl.pallas_call(
        _dft_channel_kernel,
        grid=(m // MM_TM, N_FOURIER_GROUPS),
        in_specs=[blk, _const_spec(w_cs.shape)],
        out_specs=[blk, blk],
        out_shape=[jax.ShapeDtypeStruct((m, d), BF16)] * 2,
        compiler_params=_params(("parallel", "arbitrary"), 40),
        name="dft_channels",
    )(xb, w_cs)


def _dft_seq_kernel(c_ref, s_ref, a_ref, b_ref, y_ref):
    y = jnp.dot(c_ref[...], a_ref[0], preferred_element_type=F32)
    y = y + jnp.dot(s_ref[...], b_ref[0], preferred_element_type=F32)
    y_ref[0] = y.astype(BF16)


def _dft_sequence(cos_s, neg_sin_s, a, b):
    bsz = a.shape[0]
    tn = FOURIER_CH
    blk = pl.BlockSpec((1, SEQ, tn), lambda i, j: (i, 0, j))
    return pl.pallas_call(
        _dft_seq_kernel,
        grid=(bsz, D_MODEL // tn),
        in_specs=[_const_spec(cos_s.shape), _const_spec(neg_sin_s.shape), blk, blk],
        out_specs=blk,
        out_shape=jax.ShapeDtypeStruct(a.shape, BF16),
        compiler_params=_params(("parallel", "arbitrary"), 48),
        name="dft_sequence",
    )(cos_s, neg_sin_s, a, b)


def _proj_ln_kernel(*refs, n_lhs):
    lhs = refs[:n_lhs]
    ws = refs[n_lhs:2 * n_lhs]
    res_ref, g_ref, b_ref, y_ref, yb_ref = refs[2 * n_lhs:]
    for s in range(LN_TM // SUB_TM):
        rows = slice(s * SUB_TM, (s + 1) * SUB_TM)
        acc = jnp.dot(lhs[0][rows, :], ws[0][...], preferred_element_type=F32)
        for l, w in zip(lhs[1:], ws[1:]):
            acc = acc + jnp.dot(l[rows, :], w[...], preferred_element_type=F32)
        y = _layer_norm(ALPHA * res_ref[rows, :] + acc, g_ref[...], b_ref[...])
        y_ref[rows, :] = y
        yb_ref[rows, :] = y.astype(BF16)


def _project_residual_ln(lhs_list, w, res, g, b):
    m, d = res.shape
    n = len(lhs_list)
    in_specs = [pl.BlockSpec((LN_TM, l.shape[1]), lambda i: (i, 0)) for l in lhs_list]
    k0 = 0
    for l in lhs_list:
        kb = l.shape[1]
        in_specs.append(pl.BlockSpec((kb, d), functools.partial(lambda i, blk: (blk, 0), blk=k0 // kb),
                                     pipeline_mode=pl.Buffered(1)))
        k0 += kb
    in_specs += [pl.BlockSpec((LN_TM, d), lambda i: (i, 0)), _const_spec((1, d)), _const_spec((1, d))]
    out_blk = pl.BlockSpec((LN_TM, d), lambda i: (i, 0))
    return pl.pallas_call(
        functools.partial(_proj_ln_kernel, n_lhs=n),
        grid=(m // LN_TM,),
        in_specs=in_specs,
        out_specs=[out_blk, out_blk],
        out_shape=[jax.ShapeDtypeStruct((m, d), F32), jax.ShapeDtypeStruct((m, d), BF16)],
        compiler_params=_params(("parallel",), 48),
        name="project_residual_ln",
    )(*lhs_list, *([w] * n), res, g.reshape(1, d), b.reshape(1, d))


def _ffn_kernel(xb_ref, wg_ref, wu_ref, cg_ref, cu_ref, wd_ref, g_ref, b_ref, res_hbm,
                y_hbm, yb_hbm,
                acc_ref, hg_ref, hu_ref, y_buf, yb_buf, res_sem, y_sem, yb_sem, *, nf):
    i = pl.program_id(0)
    j = pl.program_id(1)
    n_seg = SEQ // FFN_RC

    def seg_rows(c):
        return pl.ds(c * FFN_RC, FFN_RC)

    def res_copy(c):
        return pltpu.make_async_copy(res_hbm.at[i, seg_rows(c), :], y_buf.at[c % 3], res_sem.at[c % 3])

    def y_copy(c):
        return pltpu.make_async_copy(y_buf.at[c % 3], y_hbm.at[i, seg_rows(c), :], y_sem.at[c % 3])

    def yb_copy(c):
        return pltpu.make_async_copy(yb_buf.at[c % 2], yb_hbm.at[i, seg_rows(c), :], yb_sem.at[c % 2])

    def up_project(r0, nrows):
        xr = xb_ref[0, r0:r0 + nrows, :]
        hg_ref[FFN_PAD + r0:FFN_PAD + r0 + nrows, :] = jnp.dot(xr, wg_ref[...], preferred_element_type=F32)
        hu_ref[FFN_PAD + r0:FFN_PAD + r0 + nrows, :] = jnp.dot(xr, wu_ref[...], preferred_element_type=F32)

    def down_project(r0, nrows):
        def conv(h_ref, c):
            prev = h_ref[FFN_PAD + r0 - 1:FFN_PAD + r0 - 1 + nrows, :]
            cur = h_ref[FFN_PAD + r0:FFN_PAD + r0 + nrows, :]
            nxt = h_ref[FFN_PAD + r0 + 1:FFN_PAD + r0 + 1 + nrows, :]
            return prev * c[0:1] + cur * c[1:2] + nxt * c[2:3]

        gate = conv(hg_ref, cg_ref[...])
        up = conv(hu_ref, cu_ref[...])
        act = (gate * jax.nn.sigmoid(gate) * up).astype(BF16)
        return jnp.dot(act, wd_ref[...], preferred_element_type=F32)

    @pl.when(jnp.logical_and(i == 0, j == 0))
    def _():
        acc_ref[...] = jnp.zeros_like(acc_ref)
        zpad = jnp.zeros((FFN_PAD, FFN_TF), F32)
        for h_ref in (hg_ref, hu_ref):
            h_ref[0:FFN_PAD, :] = zpad
            h_ref[FFN_PAD + SEQ:, :] = zpad

    @pl.when(j < nf - 1)
    def _():
        for rb in range(SEQ // FFN_RB):
            up_project(rb * FFN_RB, FFN_RB)
        for rb in range(SEQ // FFN_RB):
            r0 = rb * FFN_RB
            acc_ref[r0:r0 + FFN_RB, :] += down_project(r0, FFN_RB)

    @pl.when(j == nf - 1)
    def _():
        up_project(0, FFN_RC)
        up_project(FFN_RC, FFN_RC)
        for s in range(n_seg + 1):
            if s >= 2:
                y_copy(s - 2).start()
                yb_copy(s - 2).start()
            if s < n_seg:
                if s >= 3:
                    y_copy(s - 3).wait()
                res_copy(s).start()
            if s >= 1:
                res_copy(s - 1).wait()
            if s >= 3:
                yb_copy(s - 3).wait()
            if s + 2 < n_seg:
                up_project((s + 2) * FFN_RC, FFN_RC)
            if s >= 1:
                c = s - 1
                rows = slice(c * FFN_RC, (c + 1) * FFN_RC)
                y = _layer_norm(ALPHA * y_buf[c % 3] + acc_ref[rows, :], g_ref[...], b_ref[...])
                y_buf[c % 3] = y
                yb_buf[c % 2] = y.astype(BF16)
                acc_ref[rows, :] = jnp.zeros((FFN_RC, D_MODEL), F32)
            if s < n_seg:
                r0 = s * FFN_RC
                acc_ref[r0:r0 + FFN_RC, :] += down_project(r0, FFN_RC)
        y_copy(n_seg - 1).start()
        yb_copy(n_seg - 1).start()
        for c in (n_seg - 3, n_seg - 2, n_seg - 1):
            y_copy(c).wait()
        for c in (n_seg - 2, n_seg - 1):
            yb_copy(c).wait()


def _ffn(xb, res, w_up, conv_w, w_down, g, b):
    bsz = xb.shape[0]
    d = D_MODEL
    nf = D_FF // FFN_TF
    return pl.pallas_call(
        functools.partial(_ffn_kernel, nf=nf),
        grid=(bsz, nf),
        in_specs=[
            pl.BlockSpec((1, SEQ, d), lambda i, j: (i, 0, 0), pipeline_mode=pl.Buffered(1)),
            pl.BlockSpec((d, FFN_TF), lambda i, j: (0, j)),
            pl.BlockSpec((d, FFN_TF), lambda i, j: (0, nf + j)),
            pl.BlockSpec((3, FFN_TF), lambda i, j: (0, j)),
            pl.BlockSpec((3, FFN_TF), lambda i, j: (0, nf + j)),
            pl.BlockSpec((FFN_TF, d), lambda i, j: (j, 0)),
            _const_spec((1, d)),
            _const_spec((1, d)),
            pl.BlockSpec(memory_space=pl.ANY),
        ],
        out_specs=[pl.BlockSpec(memory_space=pl.ANY), pl.BlockSpec(memory_space=pl.ANY)],
        out_shape=[jax.ShapeDtypeStruct((bsz, SEQ, d), F32), jax.ShapeDtypeStruct((bsz, SEQ, d), BF16)],
        scratch_shapes=[pltpu.VMEM((SEQ, d), F32),
                        pltpu.VMEM((SEQ + 2 * FFN_PAD, FFN_TF), F32),
                        pltpu.VMEM((SEQ + 2 * FFN_PAD, FFN_TF), F32),
                        pltpu.VMEM((3, FFN_RC, d), F32),
                        pltpu.VMEM((2, FFN_RC, d), BF16),
                        pltpu.SemaphoreType.DMA((3,)),
                        pltpu.SemaphoreType.DMA((3,)),
                        pltpu.SemaphoreType.DMA((2,))],
        compiler_params=_params(("arbitrary", "arbitrary"), 58),
        name="conv_ffn_ln",
    )(xb, w_up, w_up, conv_w, conv_w, w_down, g.reshape(1, d), b.reshape(1, d), res)


def _rope_tables():
    half = HEAD_DIM // 2
    inv = 1.0 / (ROPE_THETA ** (jnp.arange(half, dtype=F32) / half))
    ang = jnp.arange(SEQ, dtype=F32)[:, None] * inv[None, :]
    cos = jnp.cos(ang)
    sin = jnp.sin(ang)
    return jnp.concatenate([cos, cos], axis=-1), jnp.concatenate([-sin, sin], axis=-1)


def _trunk(x, consts, weights):
    bsz = x.shape[0]
    m = bsz * SEQ
    d = D_MODEL
    xb = x.astype(BF16)
    for l in range(DEPTH):
        i = l // 2
        x2 = x.reshape(m, d)
        xb2 = xb.reshape(m, d)
        if l % 2 == 0:
            w_in = weights["w_in_mix"][i]
            y_conv = _conv_branch(xb, w_in, weights["conv_short"][i])
            qkv = _project_qkv(xb2, w_in, consts["rope_cos"], consts["rope_sin"])
            y_attn = _attention(qkv.reshape(bsz, SEQ, 3 * ATTN_W), consts["attn_bias"])
            lhs = [y_conv.reshape(m, CONV_CH), y_attn.reshape(m, ATTN_W)]
            w_out = weights["w_out_mix"][i]
        else:
            a, b = _dft_channels(xb2, consts["dft_ch"])
            y = _dft_sequence(consts["dft_cos"], consts["dft_neg_sin"],
                              a.reshape(bsz, SEQ, d), b.reshape(bsz, SEQ, d))
            lhs = [y.reshape(m, d)]
            w_out = weights["w_out_fourier"][i]
        x2, xb2 = _project_residual_ln(lhs, w_out, x2, weights["ln_mix_g"][l], weights["ln_mix_b"][l])
        x, xb = _ffn(xb2.reshape(bsz, SEQ, d), x2.reshape(bsz, SEQ, d), weights["w_up"][l],
                     weights["conv_ffn_w"][l], weights["w_down"][l],
                     weights["ln_ffn_g"][l], weights["ln_ffn_b"][l])
    return x


def kernel(x_prompt, x_sample, w_in_mix, conv_short, w_out_mix, w_out_fourier, ln_mix_g, ln_mix_b,
           w_up, conv_ffn_w, w_down, ln_ffn_g, ln_ffn_b):
    weights = {
        "w_in_mix": w_in_mix.astype(BF16), "conv_short": conv_short,
        "w_out_mix": w_out_mix.astype(BF16), "w_out_fourier": w_out_fourier.astype(BF16),
        "ln_mix_g": ln_mix_g, "ln_mix_b": ln_mix_b,
        "w_up": w_up.astype(BF16), "conv_ffn_w": conv_ffn_w, "w_down": w_down.astype(BF16),
        "ln_ffn_g": ln_ffn_g, "ln_ffn_b": ln_ffn_b,
    }
    cos_c, sin_c = _dft_matrices(FOURIER_CH, 2.0 ** -5)
    cos_s, sin_s = _dft_matrices(SEQ, 2.0 ** -5)
    rope_cos, rope_sin = _rope_tables()
    consts = {
        "rope_cos": rope_cos, "rope_sin": rope_sin,
        "attn_bias": jnp.asarray(_attention_bias_table()),
        "dft_ch": jnp.concatenate([cos_c, sin_c], axis=1).astype(BF16),
        "dft_cos": cos_s.astype(BF16), "dft_neg_sin": (-sin_s).astype(BF16),
    }
    return (_trunk(x_prompt, consts, weights), _trunk(x_sample, consts, weights))
```

```python
import functools

import numpy as np
import jax
import jax.numpy as jnp
from jax import lax
from jax.experimental import pallas as pl
from jax.experimental.pallas import tpu as pltpu

F32 = jnp.float32
BF16 = jnp.bfloat16

D_MODEL = 2048
SEQ = 2048
DEPTH = 4
CONV_CH = 1024
N_HEADS = 8
HEAD_DIM = 128
ATTN_W = N_HEADS * HEAD_DIM
DIL_PAIRS = ((128, 1), (512, 4), (2048, 16))
ROPE_THETA = 10000.0
N_FOURIER_GROUPS = 4
FOURIER_CH = D_MODEL // N_FOURIER_GROUPS
D_FF = 5632
ALPHA = (2 * DEPTH) ** 0.25
LN_EPS = 1e-5

MIB = 1024 * 1024
NEG_BIG = -1e30

MM_TM = 1024
SUB_TM = 256
LN_TM = 512
ATT_QB = 256
GATE_TC = 256
FFN_TF = 512
FFN_RB = 512
FFN_RC = 256
FFN_PAD = 8


def _params(sem, vmem_mib):
    return pltpu.CompilerParams(dimension_semantics=sem, vmem_limit_bytes=vmem_mib * MIB)


def _const_spec(shape):
    nd = len(shape)
    return pl.BlockSpec(shape, lambda *_: (0,) * nd, pipeline_mode=pl.Buffered(1))


def _layer_spec(layer, shape):
    nd = len(shape)
    return pl.BlockSpec((None,) + tuple(shape), lambda *_: (layer,) + (0,) * nd,
                        pipeline_mode=pl.Buffered(1))


def _layer_norm(z, g, b):
    mu = jnp.mean(z, axis=-1, keepdims=True)
    zc = z - mu
    var = jnp.mean(zc * zc, axis=-1, keepdims=True)
    return zc * lax.rsqrt(var + LN_EPS) * g + b


def _shift_rows(t):
    n = t.shape[0]
    row = lax.broadcasted_iota(jnp.int32, t.shape, 0)
    prev = jnp.where(row == 0, 0.0, pltpu.roll(t, 1, axis=0))
    nxt = jnp.where(row == n - 1, 0.0, pltpu.roll(t, n - 1, axis=0))
    return prev, nxt


def _conv_branch_kernel(x_ref, wb_ref, wc_ref, wx_ref, cw_ref, o_ref):
    x = x_ref[0]
    cg = jnp.dot(x, wc_ref[...], preferred_element_type=F32)
    xv = jnp.dot(x, wx_ref[...], preferred_element_type=F32)
    t = cg * xv
    prev, nxt = _shift_rows(t)
    w = cw_ref[...]
    y = prev * w[0:1] + t * w[1:2] + nxt * w[2:3]
    bg = jnp.dot(x, wb_ref[...], preferred_element_type=F32)
    o_ref[0] = (bg * y).astype(BF16)


def _conv_branch(xb, w_in, conv_w, layer):
    bsz = xb.shape[0]
    nb = CONV_CH // GATE_TC
    return pl.pallas_call(
        _conv_branch_kernel,
        grid=(bsz, nb),
        in_specs=[pl.BlockSpec((1, SEQ, D_MODEL), lambda i, j: (i, 0, 0)),
                  pl.BlockSpec((None, D_MODEL, GATE_TC), lambda i, j: (layer, 0, j)),
                  pl.BlockSpec((None, D_MODEL, GATE_TC), lambda i, j: (layer, 0, nb + j)),
                  pl.BlockSpec((None, D_MODEL, GATE_TC), lambda i, j: (layer, 0, 2 * nb + j)),
                  pl.BlockSpec((None, 3, GATE_TC), lambda i, j: (layer, 0, j))],
        out_specs=pl.BlockSpec((1, SEQ, GATE_TC), lambda i, j: (i, 0, j)),
        out_shape=jax.ShapeDtypeStruct((bsz, SEQ, CONV_CH), BF16),
        compiler_params=_params(("parallel", "arbitrary"), 48),
        name="conv_branch",
    )(xb, w_in, w_in, w_in, conv_w)


def _qkv_kernel(x_ref, w_ref, cos_ref, sin_ref, o_ref):
    j = pl.program_id(1)

    @pl.when(j == 2)
    def _():
        o_ref[...] = jnp.dot(x_ref[...], w_ref[...], preferred_element_type=F32).astype(BF16)

    @pl.when(j < 2)
    def _():
        scale = jnp.where(j == 0, HEAD_DIM ** -0.5, 1.0).astype(F32)
        for s in range(MM_TM // SUB_TM):
            rows = slice(s * SUB_TM, (s + 1) * SUB_TM)
            acc = jnp.dot(x_ref[rows, :], w_ref[...], preferred_element_type=F32)
            cos = cos_ref[rows, :]
            sin = sin_ref[rows, :]
            for h in range(N_HEADS):
                cols = slice(h * HEAD_DIM, (h + 1) * HEAD_DIM)
                xh = acc[:, cols]
                rot = pltpu.roll(xh, HEAD_DIM // 2, axis=1)
                o_ref[rows, cols] = ((xh * cos + rot * sin) * scale).astype(BF16)


def _project_qkv(xb, w_in, layer, cos_full, sin_signed):
    m, k = xb.shape
    c0 = 3 * CONV_CH // ATTN_W
    n_pos_tiles = SEQ // MM_TM
    return pl.pallas_call(
        _qkv_kernel,
        grid=(m // MM_TM, 3),
        in_specs=[pl.BlockSpec((MM_TM, k), lambda i, j: (i, 0)),
                  pl.BlockSpec((None, k, ATTN_W), lambda i, j: (layer, 0, c0 + j)),
                  pl.BlockSpec((MM_TM, HEAD_DIM), lambda i, j: (i % n_pos_tiles, 0)),
                  pl.BlockSpec((MM_TM, HEAD_DIM), lambda i, j: (i % n_pos_tiles, 0))],
        out_specs=pl.BlockSpec((MM_TM, ATTN_W), lambda i, j: (i, j)),
        out_shape=jax.ShapeDtypeStruct((m, 3 * ATTN_W), BF16),
        compiler_params=_params(("parallel", "arbitrary"), 40),
        name="project_qkv",
    )(xb, w_in, cos_full, sin_signed)


def _attention_bias_table():
    nb = SEQ // ATT_QB
    a = np.arange(ATT_QB)[:, None]
    c = np.arange((2 * nb - 1) * ATT_QB)[None, :]
    d = c - (nb - 1) * ATT_QB - a
    count = np.zeros(d.shape, np.int64)
    for window, dilation in DIL_PAIRS:
        count += ((d % dilation == 0) & (np.abs(d) <= window // 2)).astype(np.int64)
    bias = np.where(count > 0, np.log(np.maximum(count, 1).astype(np.float64)), NEG_BIG)
    return bias.astype(np.float32)


def _attention_key_blocks(qi):
    nb = SEQ // ATT_QB
    reach = max(w // 2 for w, _ in DIL_PAIRS)
    span = -(-reach // ATT_QB)
    return max(0, qi - span), min(nb, qi + span + 1)


def _attn_kernel(q_ref, k_ref, v_ref, bias_ref, o_ref):
    nb = SEQ // ATT_QB
    for qi in range(nb):
        k0, k1 = _attention_key_blocks(qi)
        rows = slice(qi * ATT_QB, (qi + 1) * ATT_QB)
        keys = slice(k0 * ATT_QB, k1 * ATT_QB)
        q = q_ref[0, rows, :]
        s = lax.dot_general(q, k_ref[0, keys, :], (((1,), (1,)), ((), ())),
                            preferred_element_type=F32)
        b0 = (nb - 1 - qi + k0) * ATT_QB
        s = s + bias_ref[:, b0:b0 + (k1 - k0) * ATT_QB]
        m = jnp.max(s, axis=-1, keepdims=True)
        p = jnp.exp(s - m)
        den = jnp.sum(p, axis=-1, keepdims=True)
        o = jnp.dot(p.astype(BF16), v_ref[0, keys, :], preferred_element_type=F32)
        o_ref[0, rows, :] = (o / den).astype(BF16)


def _attention(qkv, bias):
    b = qkv.shape[0]
    blk = (1, SEQ, HEAD_DIM)
    return pl.pallas_call(
        _attn_kernel,
        grid=(b, N_HEADS),
        in_specs=[pl.BlockSpec(blk, lambda i, h: (i, 0, h)),
                  pl.BlockSpec(blk, lambda i, h: (i, 0, N_HEADS + h)),
                  pl.BlockSpec(blk, lambda i, h: (i, 0, 2 * N_HEADS + h)),
                  _const_spec(bias.shape)],
        out_specs=pl.BlockSpec(blk, lambda i, h: (i, 0, h)),
        out_shape=jax.ShapeDtypeStruct((b, SEQ, ATTN_W), BF16),
        compiler_params=_params(("parallel", "arbitrary"), 40),
        name="dilated_attention",
    )(qkv, qkv, qkv, bias)


HALF_SEQ = SEQ // 2
MID_ROWS = 16


def _dft_matrices(n, scale):
    idx = jnp.arange(n, dtype=jnp.int32)
    m = (idx[:, None] * idx[None, :]) % n
    ang = m.astype(F32) * (2.0 * np.pi / n)
    return jnp.cos(ang) * scale, jnp.sin(ang) * scale


def _row_reversal_matrix():
    j = np.arange(HALF_SEQ)
    r = np.zeros((HALF_SEQ, HALF_SEQ), np.float32)
    r[j, (HALF_SEQ - j) % HALF_SEQ] = 1.0
    return r


def _fourier_kernel(x_ref, wc_ref, ch_ref, sh_ref, cmid_ref, rev_ref, y_ref):
    x = x_ref[0]
    ab = jnp.dot(x, wc_ref[...], preferred_element_type=F32)
    a = ab[:, :FOURIER_CH].astype(BF16)
    b = ab[:, FOURIER_CH:].astype(BF16)
    p = jnp.dot(ch_ref[...], a, preferred_element_type=F32)
    q = jnp.dot(sh_ref[...], b, preferred_element_type=F32)
    y_ref[0, :HALF_SEQ, :] = (p - q).astype(BF16)
    mid = jnp.dot(cmid_ref[...], a, preferred_element_type=F32)
    row = lax.broadcasted_iota(jnp.int32, p.shape, 0)
    z = jnp.where(row == 0, mid[0:1, :], p + q).astype(BF16)
    y_ref[0, HALF_SEQ:, :] = jnp.dot(rev_ref[...], z, preferred_element_type=F32).astype(BF16)


def _fourier_mix(xb, consts):
    bsz = xb.shape[0]
    blk = pl.BlockSpec((1, SEQ, FOURIER_CH), lambda i, g: (i, 0, g))
    mats = [consts["dft_ch"], consts["dft_cos_half"], consts["dft_sin_half"],
            consts["dft_cos_mid"], consts["dft_row_rev"]]
    return pl.pallas_call(
        _fourier_kernel,
        grid=(bsz, N_FOURIER_GROUPS),
        in_specs=[blk] + [_const_spec(mat.shape) for mat in mats],
        out_specs=blk,
        out_shape=jax.ShapeDtypeStruct(xb.shape, BF16),
        compiler_params=_params(("parallel", "arbitrary"), 48),
        name="fourier_mix",
    )(xb, *mats)


def _proj_ln_kernel(*refs, n_lhs):
    lhs = refs[:n_lhs]
    ws = refs[n_lhs:2 * n_lhs]
    res_ref, g_ref, b_ref, y_ref, yb_ref = refs[2 * n_lhs:]
    for s in range(LN_TM // SUB_TM):
        rows = slice(s * SUB_TM, (s + 1) * SUB_TM)
        acc = jnp.dot(lhs[0][rows, :], ws[0][...], preferred_element_type=F32)
        for l, w in zip(lhs[1:], ws[1:]):
            acc = acc + jnp.dot(l[rows, :], w[...], preferred_element_type=F32)
        y = _layer_norm(ALPHA * res_ref[rows, :] + acc, g_ref[...], b_ref[...])
        y_ref[rows, :] = y
        yb_ref[rows, :] = y.astype(BF16)


def _project_residual_ln(lhs_list, w, w_layer, res, g, b, ln_layer):
    m, d = res.shape
    n = len(lhs_list)
    in_specs = [pl.BlockSpec((LN_TM, l.shape[1]), lambda i: (i, 0)) for l in lhs_list]
    k0 = 0
    for l in lhs_list:
        kb = l.shape[1]
        in_specs.append(pl.BlockSpec((None, kb, d),
                                     functools.partial(lambda i, blk: (w_layer, blk, 0), blk=k0 // kb),
                                     pipeline_mode=pl.Buffered(1)))
        k0 += kb
    in_specs += [pl.BlockSpec((LN_TM, d), lambda i: (i, 0)),
                 _layer_spec(ln_layer, (1, d)), _layer_spec(ln_layer, (1, d))]
    out_blk = pl.BlockSpec((LN_TM, d), lambda i: (i, 0))
    return pl.pallas_call(
        functools.partial(_proj_ln_kernel, n_lhs=n),
        grid=(m // LN_TM,),
        in_specs=in_specs,
        out_specs=[out_blk, out_blk],
        out_shape=[jax.ShapeDtypeStruct((m, d), F32), jax.ShapeDtypeStruct((m, d), BF16)],
        compiler_params=_params(("parallel",), 48),
        name="project_residual_ln",
    )(*lhs_list, *([w] * n), res, g.reshape(-1, 1, d), b.reshape(-1, 1, d))


def _ffn_kernel(xb_ref, wg_ref, wu_ref, cg_ref, cu_ref, wd_ref, g_ref, b_ref, res_hbm,
                y_hbm, yb_hbm,
                acc_ref, hg_ref, hu_ref, y_buf, yb_buf, res_sem, y_sem, yb_sem, *, nf):
    i = pl.program_id(0)
    j = pl.program_id(1)
    n_seg = SEQ // FFN_RC

    def seg_rows(c):
        return pl.ds(c * FFN_RC, FFN_RC)

    def res_copy(c):
        return pltpu.make_async_copy(res_hbm.at[i, seg_rows(c), :], y_buf.at[c % 3], res_sem.at[c % 3])

    def y_copy(c):
        return pltpu.make_async_copy(y_buf.at[c % 3], y_hbm.at[i, seg_rows(c), :], y_sem.at[c % 3])

    def yb_copy(c):
        return pltpu.make_async_copy(yb_buf.at[c % 2], yb_hbm.at[i, seg_rows(c), :], yb_sem.at[c % 2])

    def up_project(r0, nrows):
        xr = xb_ref[0, r0:r0 + nrows, :]
        hg_ref[FFN_PAD + r0:FFN_PAD + r0 + nrows, :] = jnp.dot(xr, wg_ref[...], preferred_element_type=F32)
        hu_ref[FFN_PAD + r0:FFN_PAD + r0 + nrows, :] = jnp.dot(xr, wu_ref[...], preferred_element_type=F32)

    def down_project(r0, nrows):
        def conv(h_ref, c):
            prev = h_ref[FFN_PAD + r0 - 1:FFN_PAD + r0 - 1 + nrows, :]
            cur = h_ref[FFN_PAD + r0:FFN_PAD + r0 + nrows, :]
            nxt = h_ref[FFN_PAD + r0 + 1:FFN_PAD + r0 + 1 + nrows, :]
            return prev * c[0:1] + cur * c[1:2] + nxt * c[2:3]

        gate = conv(hg_ref, cg_ref[...])
        up = conv(hu_ref, cu_ref[...])
        act = (gate * jax.nn.sigmoid(gate) * up).astype(BF16)
        return jnp.dot(act, wd_ref[...], preferred_element_type=F32)

    @pl.when(jnp.logical_and(i == 0, j == 0))
    def _():
        acc_ref[...] = jnp.zeros_like(acc_ref)
        zpad = jnp.zeros((FFN_PAD, FFN_TF), F32)
        for h_ref in (hg_ref, hu_ref):
            h_ref[0:FFN_PAD, :] = zpad
            h_ref[FFN_PAD + SEQ:, :] = zpad

    @pl.when(j < nf - 1)
    def _():
        for rb in range(SEQ // FFN_RB):
            up_project(rb * FFN_RB, FFN_RB)
        for rb in range(SEQ // FFN_RB):
            r0 = rb * FFN_RB
            acc_ref[r0:r0 + FFN_RB, :] += down_project(r0, FFN_RB)

    @pl.when(j == nf - 1)
    def _():
        up_project(0, FFN_RC)
        up_project(FFN_RC, FFN_RC)
        for s in range(n_seg + 1):
            if s >= 2:
                y_copy(s - 2).start()
                yb_copy(s - 2).start()
            if s < n_seg:
                if s >= 3:
                    y_copy(s - 3).wait()
                res_copy(s).start()
            if s >= 1:
                res_copy(s - 1).wait()
            if s >= 3:
                yb_copy(s - 3).wait()
            if s + 2 < n_seg:
                up_project((s + 2) * FFN_RC, FFN_RC)
            if s >= 1:
                c = s - 1
                rows = slice(c * FFN_RC, (c + 1) * FFN_RC)
                y = _layer_norm(ALPHA * y_buf[c % 3] + acc_ref[rows, :], g_ref[...], b_ref[...])
                y_buf[c % 3] = y
                yb_buf[c % 2] = y.astype(BF16)
                acc_ref[rows, :] = jnp.zeros((FFN_RC, D_MODEL), F32)
            if s < n_seg:
                r0 = s * FFN_RC
                acc_ref[r0:r0 + FFN_RC, :] += down_project(r0, FFN_RC)
        y_copy(n_seg - 1).start()
        yb_copy(n_seg - 1).start()
        for c in (n_seg - 3, n_seg - 2, n_seg - 1):
            y_copy(c).wait()
        for c in (n_seg - 2, n_seg - 1):
            yb_copy(c).wait()


def _ffn(xb, res, w_up, conv_w, w_down, g, b, layer):
    bsz = xb.shape[0]
    d = D_MODEL
    nf = D_FF // FFN_TF
    return pl.pallas_call(
        functools.partial(_ffn_kernel, nf=nf),
        grid=(bsz, nf),
        in_specs=[
            pl.BlockSpec((1, SEQ, d), lambda i, j: (i, 0, 0), pipeline_mode=pl.Buffered(1)),
            pl.BlockSpec((None, d, FFN_TF), lambda i, j: (layer, 0, j)),
            pl.BlockSpec((None, d, FFN_TF), lambda i, j: (layer, 0, nf + j)),
            pl.BlockSpec((None, 3, FFN_TF), lambda i, j: (layer, 0, j)),
            pl.BlockSpec((None, 3, FFN_TF), lambda i, j: (layer, 0, nf + j)),
            pl.BlockSpec((None, FFN_TF, d), lambda i, j: (layer, j, 0)),
            _layer_spec(layer, (1, d)),
            _layer_spec(layer, (1, d)),
            pl.BlockSpec(memory_space=pl.ANY),
        ],
        out_specs=[pl.BlockSpec(memory_space=pl.ANY), pl.BlockSpec(memory_space=pl.ANY)],
        out_shape=[jax.ShapeDtypeStruct((bsz, SEQ, d), F32), jax.ShapeDtypeStruct((bsz, SEQ, d), BF16)],
        scratch_shapes=[pltpu.VMEM((SEQ, d), F32),
                        pltpu.VMEM((SEQ + 2 * FFN_PAD, FFN_TF), F32),
                        pltpu.VMEM((SEQ + 2 * FFN_PAD, FFN_TF), F32),
                        pltpu.VMEM((3, FFN_RC, d), F32),
                        pltpu.VMEM((2, FFN_RC, d), BF16),
                        pltpu.SemaphoreType.DMA((3,)),
                        pltpu.SemaphoreType.DMA((3,)),
                        pltpu.SemaphoreType.DMA((2,))],
        compiler_params=_params(("arbitrary", "arbitrary"), 58),
        name="conv_ffn_ln",
    )(xb, w_up, w_up, conv_w, conv_w, w_down, g.reshape(-1, 1, d), b.reshape(-1, 1, d), res)


def _rope_tables():
    half = HEAD_DIM // 2
    inv = 1.0 / (ROPE_THETA ** (jnp.arange(half, dtype=F32) / half))
    ang = jnp.arange(SEQ, dtype=F32)[:, None] * inv[None, :]
    cos = jnp.cos(ang)
    sin = jnp.sin(ang)
    return jnp.concatenate([cos, cos], axis=-1), jnp.concatenate([-sin, sin], axis=-1)


def _trunk(x, consts, weights):
    bsz = x.shape[0]
    m = bsz * SEQ
    d = D_MODEL
    xb = x.astype(BF16)
    for l in range(DEPTH):
        i = l // 2
        x2 = x.reshape(m, d)
        if l % 2 == 0:
            w_in = weights["w_in_mix"]
            y_conv = _conv_branch(xb, w_in, weights["conv_short"], i)
            qkv = _project_qkv(xb.reshape(m, d), w_in, i, consts["rope_cos"], consts["rope_sin"])
            y_attn = _attention(qkv.reshape(bsz, SEQ, 3 * ATTN_W), consts["attn_bias"])
            lhs = [y_conv.reshape(m, CONV_CH), y_attn.reshape(m, ATTN_W)]
            w_out = weights["w_out_mix"]
        else:
            lhs = [_fourier_mix(xb, consts).reshape(m, d)]
            w_out = weights["w_out_fourier"]
        x2, xb2 = _project_residual_ln(lhs, w_out, i, x2, weights["ln_mix_g"], weights["ln_mix_b"], l)
        x, xb = _ffn(xb2.reshape(bsz, SEQ, d), x2.reshape(bsz, SEQ, d), weights["w_up"],
                     weights["conv_ffn_w"], weights["w_down"], weights["ln_ffn_g"], weights["ln_ffn_b"], l)
    return x


def kernel(x_prompt, x_sample, w_in_mix, conv_short, w_out_mix, w_out_fourier, ln_mix_g, ln_mix_b,
           w_up, conv_ffn_w, w_down, ln_ffn_g, ln_ffn_b):
    weights = {
        "w_in_mix": w_in_mix.astype(BF16), "conv_short": conv_short,
        "w_out_mix": w_out_mix.astype(BF16), "w_out_fourier": w_out_fourier.astype(BF16),
        "ln_mix_g": ln_mix_g, "ln_mix_b": ln_mix_b,
        "w_up": w_up.astype(BF16), "conv_ffn_w": conv_ffn_w, "w_down": w_down.astype(BF16),
        "ln_ffn_g": ln_ffn_g, "ln_ffn_b": ln_ffn_b,
    }
    cos_c, sin_c = _dft_matrices(FOURIER_CH, 2.0 ** -5)
    cos_s, sin_s = _dft_matrices(SEQ, 2.0 ** -5)
    rope_cos, rope_sin = _rope_tables()
    cos_mid = jnp.zeros((MID_ROWS, SEQ), F32).at[0].set(cos_s[HALF_SEQ])
    consts = {
        "rope_cos": rope_cos, "rope_sin": rope_sin,
        "attn_bias": jnp.asarray(_attention_bias_table()),
        "dft_ch": jnp.concatenate([cos_c, sin_c], axis=1).astype(BF16),
        "dft_cos_half": cos_s[:HALF_SEQ].astype(BF16), "dft_sin_half": sin_s[:HALF_SEQ].astype(BF16),
        "dft_cos_mid": cos_mid.astype(BF16),
        "dft_row_rev": jnp.asarray(_row_reversal_matrix()).astype(BF16),
    }
    return (_trunk(x_prompt, consts, weights), _trunk(x_sample, consts, weights))
```

```python
import functools

import numpy as np
import jax
import jax.numpy as jnp
from jax import lax
from jax.experimental import pallas as pl
from jax.experimental.pallas import tpu as pltpu

F32 = jnp.float32
BF16 = jnp.bfloat16

D_MODEL = 2048
SEQ = 2048
DEPTH = 4
CONV_CH = 1024
N_HEADS = 8
HEAD_DIM = 128
ATTN_W = N_HEADS * HEAD_DIM
DIL_PAIRS = ((128, 1), (512, 4), (2048, 16))
ROPE_THETA = 10000.0
N_FOURIER_GROUPS = 4
FOURIER_CH = D_MODEL // N_FOURIER_GROUPS
D_FF = 5632
ALPHA = (2 * DEPTH) ** 0.25
LN_EPS = 1e-5

MIB = 1024 * 1024
NEG_BIG = -1e30

MM_TM = 1024
SUB_TM = 256
LN_TM = 512
ATT_QB = 256
GATE_TC = 256
FFN_TF = 512
FFN_RB = 512
FFN_RC = 256
FFN_PAD = 8


def _params(sem, vmem_mib):
    return pltpu.CompilerParams(dimension_semantics=sem, vmem_limit_bytes=vmem_mib * MIB)


def _const_spec(shape):
    nd = len(shape)
    return pl.BlockSpec(shape, lambda *_: (0,) * nd, pipeline_mode=pl.Buffered(1))


def _layer_spec(layer, shape):
    nd = len(shape)
    return pl.BlockSpec((None,) + tuple(shape), lambda *_: (layer,) + (0,) * nd,
                        pipeline_mode=pl.Buffered(1))


def _layer_norm(z, g, b):
    mu = jnp.mean(z, axis=-1, keepdims=True)
    zc = z - mu
    var = jnp.mean(zc * zc, axis=-1, keepdims=True)
    return zc * lax.rsqrt(var + LN_EPS) * g + b


def _shift_rows(t):
    n = t.shape[0]
    row = lax.broadcasted_iota(jnp.int32, t.shape, 0)
    prev = jnp.where(row == 0, 0.0, pltpu.roll(t, 1, axis=0))
    nxt = jnp.where(row == n - 1, 0.0, pltpu.roll(t, n - 1, axis=0))
    return prev, nxt


def _conv_branch_kernel(x_ref, wb_ref, wc_ref, wx_ref, cw_ref, o_ref):
    x = x_ref[0]
    cg = jnp.dot(x, wc_ref[...], preferred_element_type=F32)
    xv = jnp.dot(x, wx_ref[...], preferred_element_type=F32)
    t = cg * xv
    prev, nxt = _shift_rows(t)
    w = cw_ref[...]
    y = prev * w[0:1] + t * w[1:2] + nxt * w[2:3]
    bg = jnp.dot(x, wb_ref[...], preferred_element_type=F32)
    o_ref[0] = (bg * y).astype(BF16)


def _conv_branch(xb, w_in, conv_w, layer):
    bsz = xb.shape[0]
    nb = CONV_CH // GATE_TC
    return pl.pallas_call(
        _conv_branch_kernel,
        grid=(bsz, nb),
        in_specs=[pl.BlockSpec((1, SEQ, D_MODEL), lambda i, j: (i, 0, 0)),
                  pl.BlockSpec((None, D_MODEL, GATE_TC), lambda i, j: (layer, 0, j)),
                  pl.BlockSpec((None, D_MODEL, GATE_TC), lambda i, j: (layer, 0, nb + j)),
                  pl.BlockSpec((None, D_MODEL, GATE_TC), lambda i, j: (layer, 0, 2 * nb + j)),
                  pl.BlockSpec((None, 3, GATE_TC), lambda i, j: (layer, 0, j))],
        out_specs=pl.BlockSpec((1, SEQ, GATE_TC), lambda i, j: (i, 0, j)),
        out_shape=jax.ShapeDtypeStruct((bsz, SEQ, CONV_CH), BF16),
        compiler_params=_params(("parallel", "arbitrary"), 48),
        name="conv_branch",
    )(xb, w_in, w_in, w_in, conv_w)


def _qkv_kernel(x_ref, w_ref, cos_ref, sin_ref, o_ref):
    j = pl.program_id(1)

    @pl.when(j == 2)
    def _():
        o_ref[...] = jnp.dot(x_ref[...], w_ref[...], preferred_element_type=F32).astype(BF16)

    @pl.when(j < 2)
    def _():
        scale = jnp.where(j == 0, HEAD_DIM ** -0.5, 1.0).astype(F32)
        for s in range(MM_TM // SUB_TM):
            rows = slice(s * SUB_TM, (s + 1) * SUB_TM)
            acc = jnp.dot(x_ref[rows, :], w_ref[...], preferred_element_type=F32)
            cos = cos_ref[rows, :]
            sin = sin_ref[rows, :]
            for h in range(N_HEADS):
                cols = slice(h * HEAD_DIM, (h + 1) * HEAD_DIM)
                xh = acc[:, cols]
                rot = pltpu.roll(xh, HEAD_DIM // 2, axis=1)
                o_ref[rows, cols] = ((xh * cos + rot * sin) * scale).astype(BF16)


def _project_qkv(xb, w_in, layer, cos_full, sin_signed):
    m, k = xb.shape
    c0 = 3 * CONV_CH // ATTN_W
    n_pos_tiles = SEQ // MM_TM
    return pl.pallas_call(
        _qkv_kernel,
        grid=(m // MM_TM, 3),
        in_specs=[pl.BlockSpec((MM_TM, k), lambda i, j: (i, 0)),
                  pl.BlockSpec((None, k, ATTN_W), lambda i, j: (layer, 0, c0 + j)),
                  pl.BlockSpec((MM_TM, HEAD_DIM), lambda i, j: (i % n_pos_tiles, 0)),
                  pl.BlockSpec((MM_TM, HEAD_DIM), lambda i, j: (i % n_pos_tiles, 0))],
        out_specs=pl.BlockSpec((MM_TM, ATTN_W), lambda i, j: (i, j)),
        out_shape=jax.ShapeDtypeStruct((m, 3 * ATTN_W), BF16),
        compiler_params=_params(("parallel", "arbitrary"), 40),
        name="project_qkv",
    )(xb, w_in, cos_full, sin_signed)


WIDE_WINDOW, WIDE_DIL = DIL_PAIRS[-1]
NARROW_PAIRS = DIL_PAIRS[:-1]
PHASE_LEN = SEQ // WIDE_DIL
GATHER_STRIDE = 4
QUARTER = SEQ // GATHER_STRIDE


def _band_bias_table():
    a = np.arange(ATT_QB)[:, None]
    c = np.arange(3 * ATT_QB)[None, :]
    d = c - ATT_QB - a
    count = np.zeros(d.shape, np.int64)
    for window, dilation in NARROW_PAIRS:
        count += ((d % dilation == 0) & (np.abs(d) <= window // 2)).astype(np.int64)
    bias = np.where(count > 0, np.log(np.maximum(count, 1).astype(np.float64)), NEG_BIG)
    return bias.astype(np.float32)


def _phase_bias_table():
    l = np.arange(PHASE_LEN)
    ok = np.abs(l[None, :] - l[:, None]) <= WIDE_WINDOW // (2 * WIDE_DIL)
    return np.where(ok, 0.0, NEG_BIG).astype(np.float32)


def _attn_kernel(q_ref, k_ref, v_ref, band_bias_ref, phase_bias_ref, o_ref,
                 nat_ref, gq_ref, gk_ref, gv_ref, go_ref, gl_ref, wide_o_ref, wide_lse_ref):
    contract_last = (((1,), (1,)), ((), ()))

    for src_ref, dst_ref in ((q_ref, gq_ref), (k_ref, gk_ref), (v_ref, gv_ref)):
        nat_ref[...] = src_ref[0].astype(F32)
        for p4 in range(GATHER_STRIDE):
            dst_ref[p4 * QUARTER:(p4 + 1) * QUARTER, :] = nat_ref[pl.ds(p4, QUARTER, stride=GATHER_STRIDE), :]

    phase_rows = [pl.ds(p4 * QUARTER + a, PHASE_LEN, stride=GATHER_STRIDE)
                  for p4 in range(GATHER_STRIDE) for a in range(GATHER_STRIDE)]
    qkv_phases = [(gq_ref[r, :].astype(BF16), gk_ref[r, :].astype(BF16), gv_ref[r, :].astype(BF16))
                  for r in phase_rows]
    phase_bias = phase_bias_ref[...]
    scores = [lax.dot_general(qp, kp, contract_last, preferred_element_type=F32) + phase_bias
              for qp, kp, _ in qkv_phases]
    probs = []
    for s in scores:
        m = jnp.max(s, axis=-1, keepdims=True)
        e = jnp.exp(s - m)
        den = jnp.sum(e, axis=-1, keepdims=True)
        probs.append((e.astype(BF16), den, m + jnp.log(den)))
    for r, (e, den, lse), (_, _, vp) in zip(phase_rows, probs, qkv_phases):
        go_ref[r, :] = jnp.dot(e, vp, preferred_element_type=F32) / den
        gl_ref[r, :] = jnp.broadcast_to(lse, (PHASE_LEN, HEAD_DIM))
    for p4 in range(GATHER_STRIDE):
        nat_rows = pl.ds(p4, QUARTER, stride=GATHER_STRIDE)
        wide_o_ref[nat_rows, :] = go_ref[p4 * QUARTER:(p4 + 1) * QUARTER, :]
        wide_lse_ref[nat_rows, :] = gl_ref[p4 * QUARTER:(p4 + 1) * QUARTER, :]

    nb = SEQ // ATT_QB

    def key_rows(qi):
        return slice(max(0, qi - 1) * ATT_QB, min(nb, qi + 2) * ATT_QB)

    def band_scores(qi):
        k0, k1 = max(0, qi - 1), min(nb, qi + 2)
        s = lax.dot_general(q_ref[0, qi * ATT_QB:(qi + 1) * ATT_QB, :], k_ref[0, key_rows(qi), :],
                            contract_last, preferred_element_type=F32)
        b0 = (k0 - qi + 1) * ATT_QB
        return s + band_bias_ref[:, b0:b0 + (k1 - k0) * ATT_QB]

    s_next = band_scores(0)
    for qi in range(nb):
        s = s_next
        if qi + 1 < nb:
            s_next = band_scores(qi + 1)
        rows = slice(qi * ATT_QB, (qi + 1) * ATT_QB)
        wide_lse = wide_lse_ref[rows, 0:1]
        m = jnp.maximum(jnp.max(s, axis=-1, keepdims=True), wide_lse)
        p = jnp.exp(s - m)
        wide_w = jnp.exp(wide_lse - m)
        den = jnp.sum(p, axis=-1, keepdims=True) + wide_w
        o = jnp.dot(p.astype(BF16), v_ref[0, key_rows(qi), :], preferred_element_type=F32)
        o_ref[0, rows, :] = ((o + wide_o_ref[rows, :] * wide_w) / den).astype(BF16)


def _attention(qkv, band_bias, phase_bias):
    b = qkv.shape[0]
    blk = (1, SEQ, HEAD_DIM)
    return pl.pallas_call(
        _attn_kernel,
        grid=(b, N_HEADS),
        in_specs=[pl.BlockSpec(blk, lambda i, h: (i, 0, h)),
                  pl.BlockSpec(blk, lambda i, h: (i, 0, N_HEADS + h)),
                  pl.BlockSpec(blk, lambda i, h: (i, 0, 2 * N_HEADS + h)),
                  _const_spec(band_bias.shape), _const_spec(phase_bias.shape)],
        out_specs=pl.BlockSpec(blk, lambda i, h: (i, 0, h)),
        out_shape=jax.ShapeDtypeStruct((b, SEQ, ATTN_W), BF16),
        scratch_shapes=[pltpu.VMEM((SEQ, HEAD_DIM), F32)] * 8,
        compiler_params=_params(("parallel", "arbitrary"), 40),
        name="dilated_attention",
    )(qkv, qkv, qkv, band_bias, phase_bias)


HALF_SEQ = SEQ // 2
MID_ROWS = 16


def _dft_matrices(n, scale):
    idx = jnp.arange(n, dtype=jnp.int32)
    m = (idx[:, None] * idx[None, :]) % n
    ang = m.astype(F32) * (2.0 * np.pi / n)
    return jnp.cos(ang) * scale, jnp.sin(ang) * scale


def _row_reversal_matrix():
    j = np.arange(HALF_SEQ)
    r = np.zeros((HALF_SEQ, HALF_SEQ), np.float32)
    r[j, (HALF_SEQ - j) % HALF_SEQ] = 1.0
    return r


def _fourier_kernel(x_ref, wc_ref, ch_ref, sh_ref, cmid_ref, rev_ref, y_ref):
    x = x_ref[0]
    ab = jnp.dot(x, wc_ref[...], preferred_element_type=F32)
    a = ab[:, :FOURIER_CH].astype(BF16)
    b = ab[:, FOURIER_CH:].astype(BF16)
    p = jnp.dot(ch_ref[...], a, preferred_element_type=F32)
    q = jnp.dot(sh_ref[...], b, preferred_element_type=F32)
    y_ref[0, :HALF_SEQ, :] = (p - q).astype(BF16)
    mid = jnp.dot(cmid_ref[...], a, preferred_element_type=F32)
    row = lax.broadcasted_iota(jnp.int32, p.shape, 0)
    z = jnp.where(row == 0, mid[0:1, :], p + q).astype(BF16)
    y_ref[0, HALF_SEQ:, :] = jnp.dot(rev_ref[...], z, preferred_element_type=F32).astype(BF16)


def _fourier_mix(xb, consts):
    bsz = xb.shape[0]
    blk = pl.BlockSpec((1, SEQ, FOURIER_CH), lambda i, g: (i, 0, g))
    mats = [consts["dft_ch"], consts["dft_cos_half"], consts["dft_sin_half"],
            consts["dft_cos_mid"], consts["dft_row_rev"]]
    return pl.pallas_call(
        _fourier_kernel,
        grid=(bsz, N_FOURIER_GROUPS),
        in_specs=[blk] + [_const_spec(mat.shape) for mat in mats],
        out_specs=blk,
        out_shape=jax.ShapeDtypeStruct(xb.shape, BF16),
        compiler_params=_params(("parallel", "arbitrary"), 48),
        name="fourier_mix",
    )(xb, *mats)


def _proj_ln_kernel(*refs, n_lhs):
    lhs = refs[:n_lhs]
    ws = refs[n_lhs:2 * n_lhs]
    res_ref, g_ref, b_ref, y_ref, yb_ref = refs[2 * n_lhs:]
    for s in range(LN_TM // SUB_TM):
        rows = slice(s * SUB_TM, (s + 1) * SUB_TM)
        acc = jnp.dot(lhs[0][rows, :], ws[0][...], preferred_element_type=F32)
        for l, w in zip(lhs[1:], ws[1:]):
            acc = acc + jnp.dot(l[rows, :], w[...], preferred_element_type=F32)
        y = _layer_norm(ALPHA * res_ref[rows, :] + acc, g_ref[...], b_ref[...])
        y_ref[rows, :] = y
        yb_ref[rows, :] = y.astype(BF16)


def _project_residual_ln(lhs_list, w, w_layer, res, g, b, ln_layer):
    m, d = res.shape
    n = len(lhs_list)
    in_specs = [pl.BlockSpec((LN_TM, l.shape[1]), lambda i: (i, 0)) for l in lhs_list]
    k0 = 0
    for l in lhs_list:
        kb = l.shape[1]
        in_specs.append(pl.BlockSpec((None, kb, d),
                                     functools.partial(lambda i, blk: (w_layer, blk, 0), blk=k0 // kb),
                                     pipeline_mode=pl.Buffered(1)))
        k0 += kb
    in_specs += [pl.BlockSpec((LN_TM, d), lambda i: (i, 0)),
                 _layer_spec(ln_layer, (1, d)), _layer_spec(ln_layer, (1, d))]
    out_blk = pl.BlockSpec((LN_TM, d), lambda i: (i, 0))
    return pl.pallas_call(
        functools.partial(_proj_ln_kernel, n_lhs=n),
        grid=(m // LN_TM,),
        in_specs=in_specs,
        out_specs=[out_blk, out_blk],
        out_shape=[jax.ShapeDtypeStruct((m, d), F32), jax.ShapeDtypeStruct((m, d), BF16)],
        compiler_params=_params(("parallel",), 48),
        name="project_residual_ln",
    )(*lhs_list, *([w] * n), res, g.reshape(-1, 1, d), b.reshape(-1, 1, d))


def _ffn_kernel(xb_ref, wg_ref, wu_ref, cg_ref, cu_ref, wd_ref, g_ref, b_ref, res_hbm,
                y_hbm, yb_hbm,
                acc_ref, hg_ref, hu_ref, y_buf, yb_buf, res_sem, y_sem, yb_sem, *, nf):
    i = pl.program_id(0)
    j = pl.program_id(1)
    n_seg = SEQ // FFN_RC

    def seg_rows(c):
        return pl.ds(c * FFN_RC, FFN_RC)

    def res_copy(c):
        return pltpu.make_async_copy(res_hbm.at[i, seg_rows(c), :], y_buf.at[c % 3], res_sem.at[c % 3])

    def y_copy(c):
        return pltpu.make_async_copy(y_buf.at[c % 3], y_hbm.at[i, seg_rows(c), :], y_sem.at[c % 3])

    def yb_copy(c):
        return pltpu.make_async_copy(yb_buf.at[c % 2], yb_hbm.at[i, seg_rows(c), :], yb_sem.at[c % 2])

    def up_project(r0, nrows):
        xr = xb_ref[0, r0:r0 + nrows, :]
        hg_ref[FFN_PAD + r0:FFN_PAD + r0 + nrows, :] = jnp.dot(xr, wg_ref[...], preferred_element_type=F32)
        hu_ref[FFN_PAD + r0:FFN_PAD + r0 + nrows, :] = jnp.dot(xr, wu_ref[...], preferred_element_type=F32)

    def down_project(r0, nrows):
        def conv(h_ref, c):
            prev = h_ref[FFN_PAD + r0 - 1:FFN_PAD + r0 - 1 + nrows, :]
            cur = h_ref[FFN_PAD + r0:FFN_PAD + r0 + nrows, :]
            nxt = h_ref[FFN_PAD + r0 + 1:FFN_PAD + r0 + 1 + nrows, :]
            return prev * c[0:1] + cur * c[1:2] + nxt * c[2:3]

        gate = conv(hg_ref, cg_ref[...])
        up = conv(hu_ref, cu_ref[...])
        act = (gate * jax.nn.sigmoid(gate) * up).astype(BF16)
        return jnp.dot(act, wd_ref[...], preferred_element_type=F32)

    @pl.when(jnp.logical_and(i == 0, j == 0))
    def _():
        acc_ref[...] = jnp.zeros_like(acc_ref)
        zpad = jnp.zeros((FFN_PAD, FFN_TF), F32)
        for h_ref in (hg_ref, hu_ref):
            h_ref[0:FFN_PAD, :] = zpad
            h_ref[FFN_PAD + SEQ:, :] = zpad

    @pl.when(j < nf - 1)
    def _():
        for rb in range(SEQ // FFN_RB):
            up_project(rb * FFN_RB, FFN_RB)
        for rb in range(SEQ // FFN_RB):
            r0 = rb * FFN_RB
            acc_ref[r0:r0 + FFN_RB, :] += down_project(r0, FFN_RB)

    @pl.when(j == nf - 1)
    def _():
        up_project(0, FFN_RC)
        up_project(FFN_RC, FFN_RC)
        for s in range(n_seg + 1):
            if s >= 2:
                y_copy(s - 2).start()
                yb_copy(s - 2).start()
            if s < n_seg:
                if s >= 3:
                    y_copy(s - 3).wait()
                res_copy(s).start()
            if s >= 1:
                res_copy(s - 1).wait()
            if s >= 3:
                yb_copy(s - 3).wait()
            if s + 2 < n_seg:
                up_project((s + 2) * FFN_RC, FFN_RC)
            if s >= 1:
                c = s - 1
                rows = slice(c * FFN_RC, (c + 1) * FFN_RC)
                y = _layer_norm(ALPHA * y_buf[c % 3] + acc_ref[rows, :], g_ref[...], b_ref[...])
                y_buf[c % 3] = y
                yb_buf[c % 2] = y.astype(BF16)
                acc_ref[rows, :] = jnp.zeros((FFN_RC, D_MODEL), F32)
            if s < n_seg:
                r0 = s * FFN_RC
                acc_ref[r0:r0 + FFN_RC, :] += down_project(r0, FFN_RC)
        y_copy(n_seg - 1).start()
        yb_copy(n_seg - 1).start()
        for c in (n_seg - 3, n_seg - 2, n_seg - 1):
            y_copy(c).wait()
        for c in (n_seg - 2, n_seg - 1):
            yb_copy(c).wait()


def _ffn(xb, res, w_up, conv_w, w_down, g, b, layer):
    bsz = xb.shape[0]
    d = D_MODEL
    nf = D_FF // FFN_TF
    return pl.pallas_call(
        functools.partial(_ffn_kernel, nf=nf),
        grid=(bsz, nf),
        in_specs=[
            pl.BlockSpec((1, SEQ, d), lambda i, j: (i, 0, 0), pipeline_mode=pl.Buffered(1)),
            pl.BlockSpec((None, d, FFN_TF), lambda i, j: (layer, 0, j)),
            pl.BlockSpec((None, d, FFN_TF), lambda i, j: (layer, 0, nf + j)),
            pl.BlockSpec((None, 3, FFN_TF), lambda i, j: (layer, 0, j)),
            pl.BlockSpec((None, 3, FFN_TF), lambda i, j: (layer, 0, nf + j)),
            pl.BlockSpec((None, FFN_TF, d), lambda i, j: (layer, j, 0)),
            _layer_spec(layer, (1, d)),
            _layer_spec(layer, (1, d)),
            pl.BlockSpec(memory_space=pl.ANY),
        ],
        out_specs=[pl.BlockSpec(memory_space=pl.ANY), pl.BlockSpec(memory_space=pl.ANY)],
        out_shape=[jax.ShapeDtypeStruct((bsz, SEQ, d), F32), jax.ShapeDtypeStruct((bsz, SEQ, d), BF16)],
        scratch_shapes=[pltpu.VMEM((SEQ, d), F32),
                        pltpu.VMEM((SEQ + 2 * FFN_PAD, FFN_TF), F32),
                        pltpu.VMEM((SEQ + 2 * FFN_PAD, FFN_TF), F32),
                        pltpu.VMEM((3, FFN_RC, d), F32),
                        pltpu.VMEM((2, FFN_RC, d), BF16),
                        pltpu.SemaphoreType.DMA((3,)),
                        pltpu.SemaphoreType.DMA((3,)),
                        pltpu.SemaphoreType.DMA((2,))],
        compiler_params=_params(("arbitrary", "arbitrary"), 58),
        name="conv_ffn_ln",
    )(xb, w_up, w_up, conv_w, conv_w, w_down, g.reshape(-1, 1, d), b.reshape(-1, 1, d), res)


def _rope_tables():
    half = HEAD_DIM // 2
    inv = 1.0 / (ROPE_THETA ** (jnp.arange(half, dtype=F32) / half))
    ang = jnp.arange(SEQ, dtype=F32)[:, None] * inv[None, :]
    cos = jnp.cos(ang)
    sin = jnp.sin(ang)
    return jnp.concatenate([cos, cos], axis=-1), jnp.concatenate([-sin, sin], axis=-1)


def _trunk(x, consts, weights):
    bsz = x.shape[0]
    m = bsz * SEQ
    d = D_MODEL
    xb = x.astype(BF16)
    for l in range(DEPTH):
        i = l // 2
        x2 = x.reshape(m, d)
        if l % 2 == 0:
            w_in = weights["w_in_mix"]
            y_conv = _conv_branch(xb, w_in, weights["conv_short"], i)
            qkv = _project_qkv(xb.reshape(m, d), w_in, i, consts["rope_cos"], consts["rope_sin"])
            y_attn = _attention(qkv.reshape(bsz, SEQ, 3 * ATTN_W), consts["band_bias"], consts["phase_bias"])
            lhs = [y_conv.reshape(m, CONV_CH), y_attn.reshape(m, ATTN_W)]
            w_out = weights["w_out_mix"]
        else:
            lhs = [_fourier_mix(xb, consts).reshape(m, d)]
            w_out = weights["w_out_fourier"]
        x2, xb2 = _project_residual_ln(lhs, w_out, i, x2, weights["ln_mix_g"], weights["ln_mix_b"], l)
        x, xb = _ffn(xb2.reshape(bsz, SEQ, d), x2.reshape(bsz, SEQ, d), weights["w_up"],
                     weights["conv_ffn_w"], weights["w_down"], weights["ln_ffn_g"], weights["ln_ffn_b"], l)
    return x


def kernel(x_prompt, x_sample, w_in_mix, conv_short, w_out_mix, w_out_fourier, ln_mix_g, ln_mix_b,
           w_up, conv_ffn_w, w_down, ln_ffn_g, ln_ffn_b):
    weights = {
        "w_in_mix": w_in_mix.astype(BF16), "conv_short": conv_short,
        "w_out_mix": w_out_mix.astype(BF16), "w_out_fourier": w_out_fourier.astype(BF16),
        "ln_mix_g": ln_mix_g, "ln_mix_b": ln_mix_b,
        "w_up": w_up.astype(BF16), "conv_ffn_w": conv_ffn_w, "w_down": w_down.astype(BF16),
        "ln_ffn_g": ln_ffn_g, "ln_ffn_b": ln_ffn_b,
    }
    cos_c, sin_c = _dft_matrices(FOURIER_CH, 2.0 ** -5)
    cos_s, sin_s = _dft_matrices(SEQ, 2.0 ** -5)
    rope_cos, rope_sin = _rope_tables()
    cos_mid = jnp.zeros((MID_ROWS, SEQ), F32).at[0].set(cos_s[HALF_SEQ])
    consts = {
        "rope_cos": rope_cos, "rope_sin": rope_sin,
        "band_bias": jnp.asarray(_band_bias_table()), "phase_bias": jnp.asarray(_phase_bias_table()),
        "dft_ch": jnp.concatenate([cos_c, sin_c], axis=1).astype(BF16),
        "dft_cos_half": cos_s[:HALF_SEQ].astype(BF16), "dft_sin_half": sin_s[:HALF_SEQ].astype(BF16),
        "dft_cos_mid": cos_mid.astype(BF16),
        "dft_row_rev": jnp.asarray(_row_reversal_matrix()).astype(BF16),
    }
    return (_trunk(x_prompt, consts, weights), _trunk(x_sample, consts, weights))
```

```python
import functools

import numpy as np
import jax
import jax.numpy as jnp
from jax import lax
from jax.experimental import pallas as pl
from jax.experimental.pallas import tpu as pltpu

F32 = jnp.float32
BF16 = jnp.bfloat16

D_MODEL = 2048
SEQ = 2048
DEPTH = 4
CONV_CH = 1024
N_HEADS = 8
HEAD_DIM = 128
ATTN_W = N_HEADS * HEAD_DIM
DIL_PAIRS = ((128, 1), (512, 4), (2048, 16))
ROPE_THETA = 10000.0
N_FOURIER_GROUPS = 4
FOURIER_CH = D_MODEL // N_FOURIER_GROUPS
D_FF = 5632
ALPHA = (2 * DEPTH) ** 0.25
LN_EPS = 1e-5

MIB = 1024 * 1024
NEG_BIG = -1e30

MM_TM = 1024
SUB_TM = 256
LN_TM = 512
ATT_QB = 256
GATE_TC = 256
FFN_TF = 512
FFN_RB = 512
FFN_RC = 256
FFN_PAD = 8


def _params(sem, vmem_mib):
    return pltpu.CompilerParams(dimension_semantics=sem, vmem_limit_bytes=vmem_mib * MIB)


def _const_spec(shape):
    nd = len(shape)
    return pl.BlockSpec(shape, lambda *_: (0,) * nd, pipeline_mode=pl.Buffered(1))


def _layer_spec(layer, shape):
    nd = len(shape)
    return pl.BlockSpec((None,) + tuple(shape), lambda *_: (layer,) + (0,) * nd,
                        pipeline_mode=pl.Buffered(1))


def _layer_norm(z, g, b):
    mu = jnp.mean(z, axis=-1, keepdims=True)
    zc = z - mu
    var = jnp.mean(zc * zc, axis=-1, keepdims=True)
    return zc * lax.rsqrt(var + LN_EPS) * g + b


def _shift_rows(t):
    n = t.shape[0]
    row = lax.broadcasted_iota(jnp.int32, t.shape, 0)
    prev = jnp.where(row == 0, 0.0, pltpu.roll(t, 1, axis=0))
    nxt = jnp.where(row == n - 1, 0.0, pltpu.roll(t, n - 1, axis=0))
    return prev, nxt


def _conv_branch_kernel(x_ref, wb_ref, wc_ref, wx_ref, cw_ref, o_ref):
    x = x_ref[0]
    cg = jnp.dot(x, wc_ref[...], preferred_element_type=F32)
    xv = jnp.dot(x, wx_ref[...], preferred_element_type=F32)
    t = cg * xv
    prev, nxt = _shift_rows(t)
    w = cw_ref[...]
    y = prev * w[0:1] + t * w[1:2] + nxt * w[2:3]
    bg = jnp.dot(x, wb_ref[...], preferred_element_type=F32)
    o_ref[0] = (bg * y).astype(BF16)


def _conv_branch(xb, w_in, conv_w, layer):
    bsz = xb.shape[0]
    nb = CONV_CH // GATE_TC
    return pl.pallas_call(
        _conv_branch_kernel,
        grid=(bsz, nb),
        in_specs=[pl.BlockSpec((1, SEQ, D_MODEL), lambda i, j: (i, 0, 0)),
                  pl.BlockSpec((None, None, D_MODEL, GATE_TC), lambda i, j: (layer, j, 0, 0)),
                  pl.BlockSpec((None, None, D_MODEL, GATE_TC), lambda i, j: (layer, nb + j, 0, 0)),
                  pl.BlockSpec((None, None, D_MODEL, GATE_TC), lambda i, j: (layer, 2 * nb + j, 0, 0)),
                  pl.BlockSpec((None, 3, GATE_TC), lambda i, j: (layer, 0, j))],
        out_specs=pl.BlockSpec((1, SEQ, GATE_TC), lambda i, j: (i, 0, j)),
        out_shape=jax.ShapeDtypeStruct((bsz, SEQ, CONV_CH), BF16),
        compiler_params=_params(("parallel", "arbitrary"), 48),
        name="conv_branch",
    )(xb, w_in, w_in, w_in, conv_w)


def _qkv_kernel(x_ref, w_ref, cos_ref, sin_ref, o_ref):
    j = pl.program_id(1)

    @pl.when(j == 2)
    def _():
        o_ref[...] = jnp.dot(x_ref[...], w_ref[...], preferred_element_type=F32).astype(BF16)

    @pl.when(j < 2)
    def _():
        scale = jnp.where(j == 0, HEAD_DIM ** -0.5, 1.0).astype(F32)
        for s in range(MM_TM // SUB_TM):
            rows = slice(s * SUB_TM, (s + 1) * SUB_TM)
            acc = jnp.dot(x_ref[rows, :], w_ref[...], preferred_element_type=F32)
            cos = cos_ref[rows, :]
            sin = sin_ref[rows, :]
            for h in range(N_HEADS):
                cols = slice(h * HEAD_DIM, (h + 1) * HEAD_DIM)
                xh = acc[:, cols]
                rot = pltpu.roll(xh, HEAD_DIM // 2, axis=1)
                o_ref[rows, cols] = ((xh * cos + rot * sin) * scale).astype(BF16)


def _project_qkv(xb, w_in, layer, cos_full, sin_signed):
    m, k = xb.shape
    c0 = 3 * CONV_CH // ATTN_W
    n_pos_tiles = SEQ // MM_TM
    return pl.pallas_call(
        _qkv_kernel,
        grid=(m // MM_TM, 3),
        in_specs=[pl.BlockSpec((MM_TM, k), lambda i, j: (i, 0)),
                  pl.BlockSpec((None, k, ATTN_W), lambda i, j: (layer, 0, c0 + j)),
                  pl.BlockSpec((MM_TM, HEAD_DIM), lambda i, j: (i % n_pos_tiles, 0)),
                  pl.BlockSpec((MM_TM, HEAD_DIM), lambda i, j: (i % n_pos_tiles, 0))],
        out_specs=pl.BlockSpec((MM_TM, ATTN_W), lambda i, j: (i, j)),
        out_shape=jax.ShapeDtypeStruct((m, 3 * ATTN_W), BF16),
        compiler_params=_params(("parallel", "arbitrary"), 40),
        name="project_qkv",
    )(xb, w_in, cos_full, sin_signed)


WIDE_WINDOW, WIDE_DIL = DIL_PAIRS[-1]
NARROW_PAIRS = DIL_PAIRS[:-1]
PHASE_LEN = SEQ // WIDE_DIL
GATHER_STRIDE = 4
QUARTER = SEQ // GATHER_STRIDE


def _band_bias_table():
    a = np.arange(ATT_QB)[:, None]
    c = np.arange(3 * ATT_QB)[None, :]
    d = c - ATT_QB - a
    count = np.zeros(d.shape, np.int64)
    for window, dilation in NARROW_PAIRS:
        count += ((d % dilation == 0) & (np.abs(d) <= window // 2)).astype(np.int64)
    bias = np.where(count > 0, np.log(np.maximum(count, 1).astype(np.float64)), NEG_BIG)
    return bias.astype(np.float32)


def _phase_bias_table():
    l = np.arange(PHASE_LEN)
    ok = np.abs(l[None, :] - l[:, None]) <= WIDE_WINDOW // (2 * WIDE_DIL)
    return np.where(ok, 0.0, NEG_BIG).astype(np.float32)


def _attn_kernel(q_ref, k_ref, v_ref, band_bias_ref, phase_bias_ref, o_ref,
                 nat_ref, gq_ref, gk_ref, gv_ref, go_ref, gl_ref, wide_o_ref, wide_lse_ref):
    contract_last = (((1,), (1,)), ((), ()))

    for src_ref, dst_ref in ((q_ref, gq_ref), (k_ref, gk_ref), (v_ref, gv_ref)):
        nat_ref[...] = src_ref[0].astype(F32)
        for p4 in range(GATHER_STRIDE):
            dst_ref[p4 * QUARTER:(p4 + 1) * QUARTER, :] = nat_ref[pl.ds(p4, QUARTER, stride=GATHER_STRIDE), :]

    phase_rows = [pl.ds(p4 * QUARTER + a, PHASE_LEN, stride=GATHER_STRIDE)
                  for p4 in range(GATHER_STRIDE) for a in range(GATHER_STRIDE)]
    qkv_phases = [(gq_ref[r, :].astype(BF16), gk_ref[r, :].astype(BF16), gv_ref[r, :].astype(BF16))
                  for r in phase_rows]
    phase_bias = phase_bias_ref[...]
    scores = [lax.dot_general(qp, kp, contract_last, preferred_element_type=F32) + phase_bias
              for qp, kp, _ in qkv_phases]
    probs = []
    for s in scores:
        m = jnp.max(s, axis=-1, keepdims=True)
        e = jnp.exp(s - m)
        den = jnp.sum(e, axis=-1, keepdims=True)
        probs.append((e.astype(BF16), den, m + jnp.log(den)))
    for r, (e, den, lse), (_, _, vp) in zip(phase_rows, probs, qkv_phases):
        go_ref[r, :] = jnp.dot(e, vp, preferred_element_type=F32) / den
        gl_ref[r, :] = jnp.broadcast_to(lse, (PHASE_LEN, HEAD_DIM))
    for p4 in range(GATHER_STRIDE):
        nat_rows = pl.ds(p4, QUARTER, stride=GATHER_STRIDE)
        wide_o_ref[nat_rows, :] = go_ref[p4 * QUARTER:(p4 + 1) * QUARTER, :]
        wide_lse_ref[nat_rows, :] = gl_ref[p4 * QUARTER:(p4 + 1) * QUARTER, :]

    nb = SEQ // ATT_QB

    def key_rows(qi):
        return slice(max(0, qi - 1) * ATT_QB, min(nb, qi + 2) * ATT_QB)

    def band_scores(qi):
        k0, k1 = max(0, qi - 1), min(nb, qi + 2)
        s = lax.dot_general(q_ref[0, qi * ATT_QB:(qi + 1) * ATT_QB, :], k_ref[0, key_rows(qi), :],
                            contract_last, preferred_element_type=F32)
        b0 = (k0 - qi + 1) * ATT_QB
        return s + band_bias_ref[:, b0:b0 + (k1 - k0) * ATT_QB]

    s_next = band_scores(0)
    for qi in range(nb):
        s = s_next
        if qi + 1 < nb:
            s_next = band_scores(qi + 1)
        rows = slice(qi * ATT_QB, (qi + 1) * ATT_QB)
        wide_lse = wide_lse_ref[rows, 0:1]
        m = jnp.maximum(jnp.max(s, axis=-1, keepdims=True), wide_lse)
        p = jnp.exp(s - m)
        wide_w = jnp.exp(wide_lse - m)
        den = jnp.sum(p, axis=-1, keepdims=True) + wide_w
        o = jnp.dot(p.astype(BF16), v_ref[0, key_rows(qi), :], preferred_element_type=F32)
        o_ref[0, rows, :] = ((o + wide_o_ref[rows, :] * wide_w) / den).astype(BF16)


def _attention(qkv, band_bias, phase_bias):
    b = qkv.shape[0]
    blk = (1, SEQ, HEAD_DIM)
    return pl.pallas_call(
        _attn_kernel,
        grid=(b, N_HEADS),
        in_specs=[pl.BlockSpec(blk, lambda i, h: (i, 0, h)),
                  pl.BlockSpec(blk, lambda i, h: (i, 0, N_HEADS + h)),
                  pl.BlockSpec(blk, lambda i, h: (i, 0, 2 * N_HEADS + h)),
                  _const_spec(band_bias.shape), _const_spec(phase_bias.shape)],
        out_specs=pl.BlockSpec(blk, lambda i, h: (i, 0, h)),
        out_shape=jax.ShapeDtypeStruct((b, SEQ, ATTN_W), BF16),
        scratch_shapes=[pltpu.VMEM((SEQ, HEAD_DIM), F32)] * 8,
        compiler_params=_params(("parallel", "arbitrary"), 40),
        name="dilated_attention",
    )(qkv, qkv, qkv, band_bias, phase_bias)


HALF_SEQ = SEQ // 2
MID_ROWS = 16


def _dft_matrices(n, scale):
    idx = jnp.arange(n, dtype=jnp.int32)
    m = (idx[:, None] * idx[None, :]) % n
    ang = m.astype(F32) * (2.0 * np.pi / n)
    return jnp.cos(ang) * scale, jnp.sin(ang) * scale


def _row_reversal_matrix():
    j = np.arange(HALF_SEQ)
    r = np.zeros((HALF_SEQ, HALF_SEQ), np.float32)
    r[j, (HALF_SEQ - j) % HALF_SEQ] = 1.0
    return r


def _fourier_kernel(x_ref, wc_ref, ch_ref, sh_ref, cmid_ref, rev_ref, y_ref):
    x = x_ref[0]
    ab = jnp.dot(x, wc_ref[...], preferred_element_type=F32)
    a = ab[:, :FOURIER_CH].astype(BF16)
    b = ab[:, FOURIER_CH:].astype(BF16)
    p = jnp.dot(ch_ref[...], a, preferred_element_type=F32)
    q = jnp.dot(sh_ref[...], b, preferred_element_type=F32)
    y_ref[0, :HALF_SEQ, :] = (p - q).astype(BF16)
    mid = jnp.dot(cmid_ref[...], a, preferred_element_type=F32)
    row = lax.broadcasted_iota(jnp.int32, p.shape, 0)
    z = jnp.where(row == 0, mid[0:1, :], p + q).astype(BF16)
    y_ref[0, HALF_SEQ:, :] = jnp.dot(rev_ref[...], z, preferred_element_type=F32).astype(BF16)


def _fourier_mix(xb, consts):
    bsz = xb.shape[0]
    blk = pl.BlockSpec((1, SEQ, FOURIER_CH), lambda i, g: (i, 0, g))
    mats = [consts["dft_ch"], consts["dft_cos_half"], consts["dft_sin_half"],
            consts["dft_cos_mid"], consts["dft_row_rev"]]
    return pl.pallas_call(
        _fourier_kernel,
        grid=(bsz, N_FOURIER_GROUPS),
        in_specs=[blk] + [_const_spec(mat.shape) for mat in mats],
        out_specs=blk,
        out_shape=jax.ShapeDtypeStruct(xb.shape, BF16),
        compiler_params=_params(("parallel", "arbitrary"), 48),
        name="fourier_mix",
    )(xb, *mats)


def _proj_ln_kernel(*refs, n_lhs):
    lhs = refs[:n_lhs]
    ws = refs[n_lhs:2 * n_lhs]
    res_ref, g_ref, b_ref, y_ref, yb_ref = refs[2 * n_lhs:]
    for s in range(LN_TM // SUB_TM):
        rows = slice(s * SUB_TM, (s + 1) * SUB_TM)
        acc = jnp.dot(lhs[0][rows, :], ws[0][...], preferred_element_type=F32)
        for l, w in zip(lhs[1:], ws[1:]):
            acc = acc + jnp.dot(l[rows, :], w[...], preferred_element_type=F32)
        y = _layer_norm(ALPHA * res_ref[rows, :] + acc, g_ref[...], b_ref[...])
        y_ref[rows, :] = y
        yb_ref[rows, :] = y.astype(BF16)


def _project_residual_ln(lhs_list, w, w_layer, res, g, b, ln_layer):
    m, d = res.shape
    n = len(lhs_list)
    in_specs = [pl.BlockSpec((LN_TM, l.shape[1]), lambda i: (i, 0)) for l in lhs_list]
    k0 = 0
    for l in lhs_list:
        kb = l.shape[1]
        in_specs.append(pl.BlockSpec((None, kb, d),
                                     functools.partial(lambda i, blk: (w_layer, blk, 0), blk=k0 // kb),
                                     pipeline_mode=pl.Buffered(1)))
        k0 += kb
    in_specs += [pl.BlockSpec((LN_TM, d), lambda i: (i, 0)),
                 _layer_spec(ln_layer, (1, d)), _layer_spec(ln_layer, (1, d))]
    out_blk = pl.BlockSpec((LN_TM, d), lambda i: (i, 0))
    return pl.pallas_call(
        functools.partial(_proj_ln_kernel, n_lhs=n),
        grid=(m // LN_TM,),
        in_specs=in_specs,
        out_specs=[out_blk, out_blk],
        out_shape=[jax.ShapeDtypeStruct((m, d), F32), jax.ShapeDtypeStruct((m, d), BF16)],
        compiler_params=_params(("parallel",), 48),
        name="project_residual_ln",
    )(*lhs_list, *([w] * n), res, g.reshape(-1, 1, d), b.reshape(-1, 1, d))


def _ffn_kernel(xb_ref, wg_ref, wu_ref, cg_ref, cu_ref, wd_ref, g_ref, b_ref, res_hbm,
                y_hbm, yb_hbm,
                acc_ref, hg_ref, hu_ref, y_buf, yb_buf, res_sem, y_sem, yb_sem, *, nf):
    i = pl.program_id(0)
    j = pl.program_id(1)
    n_seg = SEQ // FFN_RC

    def seg_rows(c):
        return pl.ds(c * FFN_RC, FFN_RC)

    def res_copy(c):
        return pltpu.make_async_copy(res_hbm.at[i, seg_rows(c), :], y_buf.at[c % 3], res_sem.at[c % 3])

    def y_copy(c):
        return pltpu.make_async_copy(y_buf.at[c % 3], y_hbm.at[i, seg_rows(c), :], y_sem.at[c % 3])

    def yb_copy(c):
        return pltpu.make_async_copy(yb_buf.at[c % 2], yb_hbm.at[i, seg_rows(c), :], yb_sem.at[c % 2])

    def up_project(r0, nrows):
        xr = xb_ref[0, r0:r0 + nrows, :]
        hg_ref[FFN_PAD + r0:FFN_PAD + r0 + nrows, :] = jnp.dot(xr, wg_ref[...], preferred_element_type=F32)
        hu_ref[FFN_PAD + r0:FFN_PAD + r0 + nrows, :] = jnp.dot(xr, wu_ref[...], preferred_element_type=F32)

    def down_project(r0, nrows):
        def conv(h_ref, c):
            prev = h_ref[FFN_PAD + r0 - 1:FFN_PAD + r0 - 1 + nrows, :]
            cur = h_ref[FFN_PAD + r0:FFN_PAD + r0 + nrows, :]
            nxt = h_ref[FFN_PAD + r0 + 1:FFN_PAD + r0 + 1 + nrows, :]
            return prev * c[0:1] + cur * c[1:2] + nxt * c[2:3]

        gate = conv(hg_ref, cg_ref[...])
        up = conv(hu_ref, cu_ref[...])
        act = (gate * jax.nn.sigmoid(gate) * up).astype(BF16)
        return jnp.dot(act, wd_ref[...], preferred_element_type=F32)

    @pl.when(jnp.logical_and(i == 0, j == 0))
    def _():
        acc_ref[...] = jnp.zeros_like(acc_ref)
        zpad = jnp.zeros((FFN_PAD, FFN_TF), F32)
        for h_ref in (hg_ref, hu_ref):
            h_ref[0:FFN_PAD, :] = zpad
            h_ref[FFN_PAD + SEQ:, :] = zpad

    @pl.when(j < nf - 1)
    def _():
        for rb in range(SEQ // FFN_RB):
            up_project(rb * FFN_RB, FFN_RB)
        for rb in range(SEQ // FFN_RB):
            r0 = rb * FFN_RB
            acc_ref[r0:r0 + FFN_RB, :] += down_project(r0, FFN_RB)

    @pl.when(j == nf - 1)
    def _():
        up_project(0, FFN_RC)
        up_project(FFN_RC, FFN_RC)
        for s in range(n_seg + 1):
            if s >= 2:
                y_copy(s - 2).start()
                yb_copy(s - 2).start()
            if s < n_seg:
                if s >= 3:
                    y_copy(s - 3).wait()
                res_copy(s).start()
            if s >= 1:
                res_copy(s - 1).wait()
            if s >= 3:
                yb_copy(s - 3).wait()
            if s + 2 < n_seg:
                up_project((s + 2) * FFN_RC, FFN_RC)
            if s >= 1:
                c = s - 1
                rows = slice(c * FFN_RC, (c + 1) * FFN_RC)
                y = _layer_norm(ALPHA * y_buf[c % 3] + acc_ref[rows, :], g_ref[...], b_ref[...])
                y_buf[c % 3] = y
                yb_buf[c % 2] = y.astype(BF16)
                acc_ref[rows, :] = jnp.zeros((FFN_RC, D_MODEL), F32)
            if s < n_seg:
                r0 = s * FFN_RC
                acc_ref[r0:r0 + FFN_RC, :] += down_project(r0, FFN_RC)
        y_copy(n_seg - 1).start()
        yb_copy(n_seg - 1).start()
        for c in (n_seg - 3, n_seg - 2, n_seg - 1):
            y_copy(c).wait()
        for c in (n_seg - 2, n_seg - 1):
            yb_copy(c).wait()


def _ffn(xb, res, w_up, conv_w, w_down, g, b, layer):
    bsz = xb.shape[0]
    d = D_MODEL
    nf = D_FF // FFN_TF
    return pl.pallas_call(
        functools.partial(_ffn_kernel, nf=nf),
        grid=(bsz, nf),
        in_specs=[
            pl.BlockSpec((1, SEQ, d), lambda i, j: (i, 0, 0), pipeline_mode=pl.Buffered(1)),
            pl.BlockSpec((None, None, d, FFN_TF), lambda i, j: (layer, j, 0, 0)),
            pl.BlockSpec((None, None, d, FFN_TF), lambda i, j: (layer, nf + j, 0, 0)),
            pl.BlockSpec((None, 3, FFN_TF), lambda i, j: (layer, 0, j)),
            pl.BlockSpec((None, 3, FFN_TF), lambda i, j: (layer, 0, nf + j)),
            pl.BlockSpec((None, FFN_TF, d), lambda i, j: (layer, j, 0)),
            _layer_spec(layer, (1, d)),
            _layer_spec(layer, (1, d)),
            pl.BlockSpec(memory_space=pl.ANY),
        ],
        out_specs=[pl.BlockSpec(memory_space=pl.ANY), pl.BlockSpec(memory_space=pl.ANY)],
        out_shape=[jax.ShapeDtypeStruct((bsz, SEQ, d), F32), jax.ShapeDtypeStruct((bsz, SEQ, d), BF16)],
        scratch_shapes=[pltpu.VMEM((SEQ, d), F32),
                        pltpu.VMEM((SEQ + 2 * FFN_PAD, FFN_TF), F32),
                        pltpu.VMEM((SEQ + 2 * FFN_PAD, FFN_TF), F32),
                        pltpu.VMEM((3, FFN_RC, d), F32),
                        pltpu.VMEM((2, FFN_RC, d), BF16),
                        pltpu.SemaphoreType.DMA((3,)),
                        pltpu.SemaphoreType.DMA((3,)),
                        pltpu.SemaphoreType.DMA((2,))],
        compiler_params=_params(("arbitrary", "arbitrary"), 58),
        name="conv_ffn_ln",
    )(xb, w_up, w_up, conv_w, conv_w, w_down, g.reshape(-1, 1, d), b.reshape(-1, 1, d), res)


def _rope_tables():
    half = HEAD_DIM // 2
    inv = 1.0 / (ROPE_THETA ** (jnp.arange(half, dtype=F32) / half))
    ang = jnp.arange(SEQ, dtype=F32)[:, None] * inv[None, :]
    cos = jnp.cos(ang)
    sin = jnp.sin(ang)
    return jnp.concatenate([cos, cos], axis=-1), jnp.concatenate([-sin, sin], axis=-1)


def _trunk(x, consts, weights):
    bsz = x.shape[0]
    m = bsz * SEQ
    d = D_MODEL
    xb = x.astype(BF16)
    for l in range(DEPTH):
        i = l // 2
        x2 = x.reshape(m, d)
        if l % 2 == 0:
            w_in = weights["w_in_mix"]
            y_conv = _conv_branch(xb, weights["w_in_conv"], weights["conv_short"], i)
            qkv = _project_qkv(xb.reshape(m, d), w_in, i, consts["rope_cos"], consts["rope_sin"])
            y_attn = _attention(qkv.reshape(bsz, SEQ, 3 * ATTN_W), consts["band_bias"], consts["phase_bias"])
            lhs = [y_conv.reshape(m, CONV_CH), y_attn.reshape(m, ATTN_W)]
            w_out = weights["w_out_mix"]
        else:
            lhs = [_fourier_mix(xb, consts).reshape(m, d)]
            w_out = weights["w_out_fourier"]
        x2, xb2 = _project_residual_ln(lhs, w_out, i, x2, weights["ln_mix_g"], weights["ln_mix_b"], l)
        x, xb = _ffn(xb2.reshape(bsz, SEQ, d), x2.reshape(bsz, SEQ, d), weights["w_up"],
                     weights["conv_ffn_w"], weights["w_down"], weights["ln_ffn_g"], weights["ln_ffn_b"], l)
    return x


def kernel(x_prompt, x_sample, w_in_mix, conv_short, w_out_mix, w_out_fourier, ln_mix_g, ln_mix_b,
           w_up, conv_ffn_w, w_down, ln_ffn_g, ln_ffn_b):
    weights = {
        "w_in_mix": w_in_mix.astype(BF16), "conv_short": conv_short,
        "w_in_conv": w_in_mix[:, :, :3 * CONV_CH].astype(BF16).reshape(
            -1, D_MODEL, 3 * CONV_CH // GATE_TC, GATE_TC).transpose(0, 2, 1, 3),
        "w_out_mix": w_out_mix.astype(BF16), "w_out_fourier": w_out_fourier.astype(BF16),
        "ln_mix_g": ln_mix_g, "ln_mix_b": ln_mix_b,
        "w_up": w_up.astype(BF16).reshape(DEPTH, D_MODEL, 2 * D_FF // FFN_TF, FFN_TF).transpose(0, 2, 1, 3),
        "conv_ffn_w": conv_ffn_w, "w_down": w_down.astype(BF16),
        "ln_ffn_g": ln_ffn_g, "ln_ffn_b": ln_ffn_b,
    }
    cos_c, sin_c = _dft_matrices(FOURIER_CH, 2.0 ** -5)
    cos_s, sin_s = _dft_matrices(SEQ, 2.0 ** -5)
    rope_cos, rope_sin = _rope_tables()
    cos_mid = jnp.zeros((MID_ROWS, SEQ), F32).at[0].set(cos_s[HALF_SEQ])
    consts = {
        "rope_cos": rope_cos, "rope_sin": rope_sin,
        "band_bias": jnp.asarray(_band_bias_table()), "phase_bias": jnp.asarray(_phase_bias_table()),
        "dft_ch": jnp.concatenate([cos_c, sin_c], axis=1).astype(BF16),
        "dft_cos_half": cos_s[:HALF_SEQ].astype(BF16), "dft_sin_half": sin_s[:HALF_SEQ].astype(BF16),
        "dft_cos_mid": cos_mid.astype(BF16),
        "dft_row_rev": jnp.asarray(_row_reversal_matrix()).astype(BF16),
    }
    return (_trunk(x_prompt, consts, weights), _trunk(x_sample, consts, weights))
```

```python
import functools

import numpy as np
import jax
import jax.numpy as jnp
from jax import lax
from jax.experimental import pallas as pl
from jax.experimental.pallas import tpu as pltpu

F32 = jnp.float32
BF16 = jnp.bfloat16

D_MODEL = 2048
SEQ = 2048
DEPTH = 4
CONV_CH = 1024
N_HEADS = 8
HEAD_DIM = 128
ATTN_W = N_HEADS * HEAD_DIM
DIL_PAIRS = ((128, 1), (512, 4), (2048, 16))
ROPE_THETA = 10000.0
N_FOURIER_GROUPS = 4
FOURIER_CH = D_MODEL // N_FOURIER_GROUPS
D_FF = 5632
ALPHA = (2 * DEPTH) ** 0.25
LN_EPS = 1e-5

MIB = 1024 * 1024
NEG_BIG = -1e30

MM_TM = 1024
SUB_TM = 256
LN_TM = 512
ATT_QB = 256
GATE_TC = 256
FFN_TF = 512
FFN_RB = 512
FFN_RC = 256
FFN_PAD = 8


def _params(sem, vmem_mib):
    return pltpu.CompilerParams(dimension_semantics=sem, vmem_limit_bytes=vmem_mib * MIB)


def _const_spec(shape):
    nd = len(shape)
    return pl.BlockSpec(shape, lambda *_: (0,) * nd, pipeline_mode=pl.Buffered(1))


def _layer_spec(layer, shape):
    nd = len(shape)
    return pl.BlockSpec((None,) + tuple(shape), lambda *_: (layer,) + (0,) * nd,
                        pipeline_mode=pl.Buffered(1))


def _layer_norm(z, g, b):
    mu = jnp.mean(z, axis=-1, keepdims=True)
    zc = z - mu
    var = jnp.mean(zc * zc, axis=-1, keepdims=True)
    return zc * lax.rsqrt(var + LN_EPS) * g + b


def _shift_rows(t):
    n = t.shape[0]
    row = lax.broadcasted_iota(jnp.int32, t.shape, 0)
    prev = jnp.where(row == 0, 0.0, pltpu.roll(t, 1, axis=0))
    nxt = jnp.where(row == n - 1, 0.0, pltpu.roll(t, n - 1, axis=0))
    return prev, nxt


def _conv_branch_kernel(x_ref, wb_ref, wc_ref, wx_ref, cw_ref, o_ref):
    x = x_ref[0]
    cg = jnp.dot(x, wc_ref[...], preferred_element_type=F32)
    xv = jnp.dot(x, wx_ref[...], preferred_element_type=F32)
    t = cg * xv
    prev, nxt = _shift_rows(t)
    w = cw_ref[...]
    y = prev * w[0:1] + t * w[1:2] + nxt * w[2:3]
    bg = jnp.dot(x, wb_ref[...], preferred_element_type=F32)
    o_ref[0] = (bg * y).astype(BF16)


def _conv_branch(xb, w_in, conv_w, layer):
    bsz = xb.shape[0]
    nb = CONV_CH // GATE_TC
    return pl.pallas_call(
        _conv_branch_kernel,
        grid=(bsz, nb),
        in_specs=[pl.BlockSpec((1, SEQ, D_MODEL), lambda i, j: (i, 0, 0)),
                  pl.BlockSpec((None, D_MODEL, GATE_TC), lambda i, j: (layer, 0, j)),
                  pl.BlockSpec((None, D_MODEL, GATE_TC), lambda i, j: (layer, 0, nb + j)),
                  pl.BlockSpec((None, D_MODEL, GATE_TC), lambda i, j: (layer, 0, 2 * nb + j)),
                  pl.BlockSpec((None, 3, GATE_TC), lambda i, j: (layer, 0, j))],
        out_specs=pl.BlockSpec((1, SEQ, GATE_TC), lambda i, j: (i, 0, j)),
        out_shape=jax.ShapeDtypeStruct((bsz, SEQ, CONV_CH), BF16),
        compiler_params=_params(("parallel", "arbitrary"), 48),
        name="conv_branch",
    )(xb, w_in, w_in, w_in, conv_w)


def _qkv_kernel(x_ref, w_ref, cos_ref, sin_ref, o_ref):
    j = pl.program_id(1)

    @pl.when(j == 2)
    def _():
        acc = jnp.dot(x_ref[...], w_ref[...], preferred_element_type=F32)
        for h in range(N_HEADS):
            o_ref[h] = acc[:, h * HEAD_DIM:(h + 1) * HEAD_DIM].astype(BF16)

    @pl.when(j < 2)
    def _():
        scale = jnp.where(j == 0, HEAD_DIM ** -0.5, 1.0).astype(F32)
        for s in range(MM_TM // SUB_TM):
            rows = slice(s * SUB_TM, (s + 1) * SUB_TM)
            acc = jnp.dot(x_ref[rows, :], w_ref[...], preferred_element_type=F32)
            cos = cos_ref[rows, :]
            sin = sin_ref[rows, :]
            for h in range(N_HEADS):
                xh = acc[:, h * HEAD_DIM:(h + 1) * HEAD_DIM]
                rot = pltpu.roll(xh, HEAD_DIM // 2, axis=1)
                o_ref[h, rows, :] = ((xh * cos + rot * sin) * scale).astype(BF16)


def _project_qkv(xb, w_in, layer, cos_full, sin_signed):
    m, k = xb.shape
    c0 = 3 * CONV_CH // ATTN_W
    n_pos_tiles = SEQ // MM_TM
    return pl.pallas_call(
        _qkv_kernel,
        grid=(m // MM_TM, 3),
        in_specs=[pl.BlockSpec((MM_TM, k), lambda i, j: (i, 0)),
                  pl.BlockSpec((None, k, ATTN_W), lambda i, j: (layer, 0, c0 + j)),
                  pl.BlockSpec((MM_TM, HEAD_DIM), lambda i, j: (i % n_pos_tiles, 0)),
                  pl.BlockSpec((MM_TM, HEAD_DIM), lambda i, j: (i % n_pos_tiles, 0))],
        out_specs=pl.BlockSpec((N_HEADS, MM_TM, HEAD_DIM), lambda i, j: (j, i, 0)),
        out_shape=jax.ShapeDtypeStruct((3 * N_HEADS, m, HEAD_DIM), BF16),
        compiler_params=_params(("parallel", "arbitrary"), 40),
        name="project_qkv",
    )(xb, w_in, cos_full, sin_signed)


WIDE_WINDOW, WIDE_DIL = DIL_PAIRS[-1]
NARROW_PAIRS = DIL_PAIRS[:-1]
PHASE_LEN = SEQ // WIDE_DIL
GATHER_STRIDE = 4
QUARTER = SEQ // GATHER_STRIDE


def _band_bias_table():
    a = np.arange(ATT_QB)[:, None]
    c = np.arange(3 * ATT_QB)[None, :]
    d = c - ATT_QB - a
    count = np.zeros(d.shape, np.int64)
    for window, dilation in NARROW_PAIRS:
        count += ((d % dilation == 0) & (np.abs(d) <= window // 2)).astype(np.int64)
    bias = np.where(count > 0, np.log(np.maximum(count, 1).astype(np.float64)), NEG_BIG)
    return bias.astype(np.float32)


def _phase_bias_table():
    l = np.arange(PHASE_LEN)
    ok = np.abs(l[None, :] - l[:, None]) <= WIDE_WINDOW // (2 * WIDE_DIL)
    return np.where(ok, 0.0, NEG_BIG).astype(np.float32)


def _attn_kernel(q_ref, k_ref, v_ref, band_bias_ref, phase_bias_ref, o_ref,
                 nat_ref, gq_ref, gk_ref, gv_ref, go_ref, gl_ref, wide_o_ref, wide_lse_ref):
    contract_last = (((1,), (1,)), ((), ()))

    for src_ref, dst_ref in ((q_ref, gq_ref), (k_ref, gk_ref), (v_ref, gv_ref)):
        nat_ref[...] = src_ref[...].astype(F32)
        for p4 in range(GATHER_STRIDE):
            dst_ref[p4 * QUARTER:(p4 + 1) * QUARTER, :] = nat_ref[pl.ds(p4, QUARTER, stride=GATHER_STRIDE), :]

    phase_rows = [pl.ds(p4 * QUARTER + a, PHASE_LEN, stride=GATHER_STRIDE)
                  for p4 in range(GATHER_STRIDE) for a in range(GATHER_STRIDE)]
    qkv_phases = [(gq_ref[r, :].astype(BF16), gk_ref[r, :].astype(BF16), gv_ref[r, :].astype(BF16))
                  for r in phase_rows]
    phase_bias = phase_bias_ref[...]
    scores = [lax.dot_general(qp, kp, contract_last, preferred_element_type=F32) + phase_bias
              for qp, kp, _ in qkv_phases]
    probs = []
    for s in scores:
        m = jnp.max(s, axis=-1, keepdims=True)
        e = jnp.exp(s - m)
        den = jnp.sum(e, axis=-1, keepdims=True)
        probs.append((e.astype(BF16), den, m + jnp.log(den)))
    for r, (e, den, lse), (_, _, vp) in zip(phase_rows, probs, qkv_phases):
        go_ref[r, :] = jnp.dot(e, vp, preferred_element_type=F32) / den
        gl_ref[r, :] = jnp.broadcast_to(lse, (PHASE_LEN, HEAD_DIM))
    for p4 in range(GATHER_STRIDE):
        nat_rows = pl.ds(p4, QUARTER, stride=GATHER_STRIDE)
        wide_o_ref[nat_rows, :] = go_ref[p4 * QUARTER:(p4 + 1) * QUARTER, :]
        wide_lse_ref[nat_rows, :] = gl_ref[p4 * QUARTER:(p4 + 1) * QUARTER, :]

    nb = SEQ // ATT_QB

    def key_rows(qi):
        return slice(max(0, qi - 1) * ATT_QB, min(nb, qi + 2) * ATT_QB)

    def band_scores(qi):
        k0, k1 = max(0, qi - 1), min(nb, qi + 2)
        s = lax.dot_general(q_ref[qi * ATT_QB:(qi + 1) * ATT_QB, :], k_ref[key_rows(qi), :],
                            contract_last, preferred_element_type=F32)
        b0 = (k0 - qi + 1) * ATT_QB
        return s + band_bias_ref[:, b0:b0 + (k1 - k0) * ATT_QB]

    s_next = band_scores(0)
    for qi in range(nb):
        s = s_next
        if qi + 1 < nb:
            s_next = band_scores(qi + 1)
        rows = slice(qi * ATT_QB, (qi + 1) * ATT_QB)
        wide_lse = wide_lse_ref[rows, 0:1]
        m = jnp.maximum(jnp.max(s, axis=-1, keepdims=True), wide_lse)
        p = jnp.exp(s - m)
        wide_w = jnp.exp(wide_lse - m)
        den = jnp.sum(p, axis=-1, keepdims=True) + wide_w
        o = jnp.dot(p.astype(BF16), v_ref[key_rows(qi), :], preferred_element_type=F32)
        o_ref[0, rows, :] = ((o + wide_o_ref[rows, :] * wide_w) / den).astype(BF16)


def _attention(qkv, band_bias, phase_bias):
    b = qkv.shape[1]
    head_blk = (None, None, SEQ, HEAD_DIM)
    return pl.pallas_call(
        _attn_kernel,
        grid=(b, N_HEADS),
        in_specs=[pl.BlockSpec(head_blk, lambda i, h: (h, i, 0, 0)),
                  pl.BlockSpec(head_blk, lambda i, h: (N_HEADS + h, i, 0, 0)),
                  pl.BlockSpec(head_blk, lambda i, h: (2 * N_HEADS + h, i, 0, 0)),
                  _const_spec(band_bias.shape), _const_spec(phase_bias.shape)],
        out_specs=pl.BlockSpec((1, SEQ, HEAD_DIM), lambda i, h: (i, 0, h)),
        out_shape=jax.ShapeDtypeStruct((b, SEQ, ATTN_W), BF16),
        scratch_shapes=[pltpu.VMEM((SEQ, HEAD_DIM), F32)] * 8,
        compiler_params=_params(("parallel", "arbitrary"), 40),
        name="dilated_attention",
    )(qkv, qkv, qkv, band_bias, phase_bias)


HALF_SEQ = SEQ // 2
MID_ROWS = 16


def _dft_matrices(n, scale):
    idx = jnp.arange(n, dtype=jnp.int32)
    m = (idx[:, None] * idx[None, :]) % n
    ang = m.astype(F32) * (2.0 * np.pi / n)
    return jnp.cos(ang) * scale, jnp.sin(ang) * scale


def _row_reversal_matrix():
    j = np.arange(HALF_SEQ)
    r = np.zeros((HALF_SEQ, HALF_SEQ), np.float32)
    r[j, (HALF_SEQ - j) % HALF_SEQ] = 1.0
    return r


def _fourier_kernel(x_ref, wc_ref, ch_ref, sh_ref, cmid_ref, rev_ref, y_ref):
    x = x_ref[0]
    ab = jnp.dot(x, wc_ref[...], preferred_element_type=F32)
    a = ab[:, :FOURIER_CH].astype(BF16)
    b = ab[:, FOURIER_CH:].astype(BF16)
    p = jnp.dot(ch_ref[...], a, preferred_element_type=F32)
    q = jnp.dot(sh_ref[...], b, preferred_element_type=F32)
    y_ref[0, :HALF_SEQ, :] = (p - q).astype(BF16)
    mid = jnp.dot(cmid_ref[...], a, preferred_element_type=F32)
    row = lax.broadcasted_iota(jnp.int32, p.shape, 0)
    z = jnp.where(row == 0, mid[0:1, :], p + q).astype(BF16)
    y_ref[0, HALF_SEQ:, :] = jnp.dot(rev_ref[...], z, preferred_element_type=F32).astype(BF16)


def _fourier_mix(xb, consts):
    bsz = xb.shape[0]
    blk = pl.BlockSpec((1, SEQ, FOURIER_CH), lambda i, g: (i, 0, g))
    mats = [consts["dft_ch"], consts["dft_cos_half"], consts["dft_sin_half"],
            consts["dft_cos_mid"], consts["dft_row_rev"]]
    return pl.pallas_call(
        _fourier_kernel,
        grid=(bsz, N_FOURIER_GROUPS),
        in_specs=[blk] + [_const_spec(mat.shape) for mat in mats],
        out_specs=blk,
        out_shape=jax.ShapeDtypeStruct(xb.shape, BF16),
        compiler_params=_params(("parallel", "arbitrary"), 48),
        name="fourier_mix",
    )(xb, *mats)


def _proj_ln_kernel(*refs, n_lhs):
    lhs = refs[:n_lhs]
    ws = refs[n_lhs:2 * n_lhs]
    res_ref, g_ref, b_ref, y_ref, yb_ref = refs[2 * n_lhs:]
    for s in range(LN_TM // SUB_TM):
        rows = slice(s * SUB_TM, (s + 1) * SUB_TM)
        acc = jnp.dot(lhs[0][rows, :], ws[0][...], preferred_element_type=F32)
        for l, w in zip(lhs[1:], ws[1:]):
            acc = acc + jnp.dot(l[rows, :], w[...], preferred_element_type=F32)
        y = _layer_norm(ALPHA * res_ref[rows, :] + acc, g_ref[...], b_ref[...])
        y_ref[rows, :] = y
        yb_ref[rows, :] = y.astype(BF16)


def _project_residual_ln(lhs_list, w, w_layer, res, g, b, ln_layer):
    m, d = res.shape
    n = len(lhs_list)
    in_specs = [pl.BlockSpec((LN_TM, l.shape[1]), lambda i: (i, 0)) for l in lhs_list]
    k0 = 0
    for l in lhs_list:
        kb = l.shape[1]
        in_specs.append(pl.BlockSpec((None, kb, d),
                                     functools.partial(lambda i, blk: (w_layer, blk, 0), blk=k0 // kb),
                                     pipeline_mode=pl.Buffered(1)))
        k0 += kb
    in_specs += [pl.BlockSpec((LN_TM, d), lambda i: (i, 0)),
                 _layer_spec(ln_layer, (1, d)), _layer_spec(ln_layer, (1, d))]
    out_blk = pl.BlockSpec((LN_TM, d), lambda i: (i, 0))
    return pl.pallas_call(
        functools.partial(_proj_ln_kernel, n_lhs=n),
        grid=(m // LN_TM,),
        in_specs=in_specs,
        out_specs=[out_blk, out_blk],
        out_shape=[jax.ShapeDtypeStruct((m, d), F32), jax.ShapeDtypeStruct((m, d), BF16)],
        compiler_params=_params(("parallel",), 48),
        name="project_residual_ln",
    )(*lhs_list, *([w] * n), res, g.reshape(-1, 1, d), b.reshape(-1, 1, d))


def _ffn_kernel(xb_ref, wg_ref, wu_ref, cg_ref, cu_ref, wd_ref, g_ref, b_ref, res_hbm,
                y_hbm, yb_hbm,
                acc_ref, hg_ref, hu_ref, y_buf, yb_buf, res_sem, y_sem, yb_sem, *, nf):
    i = pl.program_id(0)
    j = pl.program_id(1)
    n_seg = SEQ // FFN_RC

    def seg_rows(c):
        return pl.ds(c * FFN_RC, FFN_RC)

    def res_copy(c):
        return pltpu.make_async_copy(res_hbm.at[i, seg_rows(c), :], y_buf.at[c % 3], res_sem.at[c % 3])

    def y_copy(c):
        return pltpu.make_async_copy(y_buf.at[c % 3], y_hbm.at[i, seg_rows(c), :], y_sem.at[c % 3])

    def yb_copy(c):
        return pltpu.make_async_copy(yb_buf.at[c % 2], yb_hbm.at[i, seg_rows(c), :], yb_sem.at[c % 2])

    def up_project(r0, nrows):
        xr = xb_ref[0, r0:r0 + nrows, :]
        hg_ref[FFN_PAD + r0:FFN_PAD + r0 + nrows, :] = jnp.dot(xr, wg_ref[...], preferred_element_type=F32)
        hu_ref[FFN_PAD + r0:FFN_PAD + r0 + nrows, :] = jnp.dot(xr, wu_ref[...], preferred_element_type=F32)

    def down_project(r0, nrows):
        def conv(h_ref, c):
            prev = h_ref[FFN_PAD + r0 - 1:FFN_PAD + r0 - 1 + nrows, :]
            cur = h_ref[FFN_PAD + r0:FFN_PAD + r0 + nrows, :]
            nxt = h_ref[FFN_PAD + r0 + 1:FFN_PAD + r0 + 1 + nrows, :]
            return prev * c[0:1] + cur * c[1:2] + nxt * c[2:3]

        gate = conv(hg_ref, cg_ref[...])
        up = conv(hu_ref, cu_ref[...])
        act = (gate * jax.nn.sigmoid(gate) * up).astype(BF16)
        return jnp.dot(act, wd_ref[...], preferred_element_type=F32)

    @pl.when(jnp.logical_and(i == 0, j == 0))
    def _():
        acc_ref[...] = jnp.zeros_like(acc_ref)
        zpad = jnp.zeros((FFN_PAD, FFN_TF), F32)
        for h_ref in (hg_ref, hu_ref):
            h_ref[0:FFN_PAD, :] = zpad
            h_ref[FFN_PAD + SEQ:, :] = zpad

    @pl.when(j < nf - 1)
    def _():
        for rb in range(SEQ // FFN_RB):
            up_project(rb * FFN_RB, FFN_RB)
        for rb in range(SEQ // FFN_RB):
            r0 = rb * FFN_RB
            acc_ref[r0:r0 + FFN_RB, :] += down_project(r0, FFN_RB)

    @pl.when(j == nf - 1)
    def _():
        up_project(0, FFN_RC)
        up_project(FFN_RC, FFN_RC)
        for s in range(n_seg + 1):
            if s >= 2:
                y_copy(s - 2).start()
                yb_copy(s - 2).start()
            if s < n_seg:
                if s >= 3:
                    y_copy(s - 3).wait()
                res_copy(s).start()
            if s >= 1:
                res_copy(s - 1).wait()
            if s >= 3:
                yb_copy(s - 3).wait()
            if s + 2 < n_seg:
                up_project((s + 2) * FFN_RC, FFN_RC)
            if s >= 1:
                c = s - 1
                rows = slice(c * FFN_RC, (c + 1) * FFN_RC)
                y = _layer_norm(ALPHA * y_buf[c % 3] + acc_ref[rows, :], g_ref[...], b_ref[...])
                y_buf[c % 3] = y
                yb_buf[c % 2] = y.astype(BF16)
                acc_ref[rows, :] = jnp.zeros((FFN_RC, D_MODEL), F32)
            if s < n_seg:
                r0 = s * FFN_RC
                acc_ref[r0:r0 + FFN_RC, :] += down_project(r0, FFN_RC)
        y_copy(n_seg - 1).start()
        yb_copy(n_seg - 1).start()
        for c in (n_seg - 3, n_seg - 2, n_seg - 1):
            y_copy(c).wait()
        for c in (n_seg - 2, n_seg - 1):
            yb_copy(c).wait()


def _ffn(xb, res, w_up, conv_w, w_down, g, b, layer):
    bsz = xb.shape[0]
    d = D_MODEL
    nf = D_FF // FFN_TF
    return pl.pallas_call(
        functools.partial(_ffn_kernel, nf=nf),
        grid=(bsz, nf),
        in_specs=[
            pl.BlockSpec((1, SEQ, d), lambda i, j: (i, 0, 0), pipeline_mode=pl.Buffered(1)),
            pl.BlockSpec((None, d, FFN_TF), lambda i, j: (layer, 0, j)),
            pl.BlockSpec((None, d, FFN_TF), lambda i, j: (layer, 0, nf + j)),
            pl.BlockSpec((None, 3, FFN_TF), lambda i, j: (layer, 0, j)),
            pl.BlockSpec((None, 3, FFN_TF), lambda i, j: (layer, 0, nf + j)),
            pl.BlockSpec((None, FFN_TF, d), lambda i, j: (layer, j, 0)),
            _layer_spec(layer, (1, d)),
            _layer_spec(layer, (1, d)),
            pl.BlockSpec(memory_space=pl.ANY),
        ],
        out_specs=[pl.BlockSpec(memory_space=pl.ANY), pl.BlockSpec(memory_space=pl.ANY)],
        out_shape=[jax.ShapeDtypeStruct((bsz, SEQ, d), F32), jax.ShapeDtypeStruct((bsz, SEQ, d), BF16)],
        scratch_shapes=[pltpu.VMEM((SEQ, d), F32),
                        pltpu.VMEM((SEQ + 2 * FFN_PAD, FFN_TF), F32),
                        pltpu.VMEM((SEQ + 2 * FFN_PAD, FFN_TF), F32),
                        pltpu.VMEM((3, FFN_RC, d), F32),
                        pltpu.VMEM((2, FFN_RC, d), BF16),
                        pltpu.SemaphoreType.DMA((3,)),
                        pltpu.SemaphoreType.DMA((3,)),
                        pltpu.SemaphoreType.DMA((2,))],
        compiler_params=_params(("arbitrary", "arbitrary"), 58),
        name="conv_ffn_ln",
    )(xb, w_up, w_up, conv_w, conv_w, w_down, g.reshape(-1, 1, d), b.reshape(-1, 1, d), res)


def _rope_tables():
    half = HEAD_DIM // 2
    inv = 1.0 / (ROPE_THETA ** (jnp.arange(half, dtype=F32) / half))
    ang = jnp.arange(SEQ, dtype=F32)[:, None] * inv[None, :]
    cos = jnp.cos(ang)
    sin = jnp.sin(ang)
    return jnp.concatenate([cos, cos], axis=-1), jnp.concatenate([-sin, sin], axis=-1)


def _trunk(x, consts, weights):
    bsz = x.shape[0]
    m = bsz * SEQ
    d = D_MODEL
    xb = x.astype(BF16)
    for l in range(DEPTH):
        i = l // 2
        x2 = x.reshape(m, d)
        if l % 2 == 0:
            w_in = weights["w_in_mix"]
            y_conv = _conv_branch(xb, w_in, weights["conv_short"], i)
            qkv = _project_qkv(xb.reshape(m, d), w_in, i, consts["rope_cos"], consts["rope_sin"])
            y_attn = _attention(qkv.reshape(3 * N_HEADS, bsz, SEQ, HEAD_DIM),
                                consts["band_bias"], consts["phase_bias"])
            lhs = [y_conv.reshape(m, CONV_CH), y_attn.reshape(m, ATTN_W)]
            w_out = weights["w_out_mix"]
        else:
            lhs = [_fourier_mix(xb, consts).reshape(m, d)]
            w_out = weights["w_out_fourier"]
        x2, xb2 = _project_residual_ln(lhs, w_out, i, x2, weights["ln_mix_g"], weights["ln_mix_b"], l)
        x, xb = _ffn(xb2.reshape(bsz, SEQ, d), x2.reshape(bsz, SEQ, d), weights["w_up"],
                     weights["conv_ffn_w"], weights["w_down"], weights["ln_ffn_g"], weights["ln_ffn_b"], l)
    return x


def kernel(x_prompt, x_sample, w_in_mix, conv_short, w_out_mix, w_out_fourier, ln_mix_g, ln_mix_b,
           w_up, conv_ffn_w, w_down, ln_ffn_g, ln_ffn_b):
    weights = {
        "w_in_mix": w_in_mix.astype(BF16), "conv_short": conv_short,
        "w_out_mix": w_out_mix.astype(BF16), "w_out_fourier": w_out_fourier.astype(BF16),
        "ln_mix_g": ln_mix_g, "ln_mix_b": ln_mix_b,
        "w_up": w_up.astype(BF16), "conv_ffn_w": conv_ffn_w, "w_down": w_down.astype(BF16),
        "ln_ffn_g": ln_ffn_g, "ln_ffn_b": ln_ffn_b,
    }
    cos_c, sin_c = _dft_matrices(FOURIER_CH, 2.0 ** -5)
    cos_s, sin_s = _dft_matrices(SEQ, 2.0 ** -5)
    rope_cos, rope_sin = _rope_tables()
    cos_mid = jnp.zeros((MID_ROWS, SEQ), F32).at[0].set(cos_s[HALF_SEQ])
    consts = {
        "rope_cos": rope_cos, "rope_sin": rope_sin,
        "band_bias": jnp.asarray(_band_bias_table()), "phase_bias": jnp.asarray(_phase_bias_table()),
        "dft_ch": jnp.concatenate([cos_c, sin_c], axis=1).astype(BF16),
        "dft_cos_half": cos_s[:HALF_SEQ].astype(BF16), "dft_sin_half": sin_s[:HALF_SEQ].astype(BF16),
        "dft_cos_mid": cos_mid.astype(BF16),
        "dft_row_rev": jnp.asarray(_row_reversal_matrix()).astype(BF16),
    }
    return (_trunk(x_prompt, consts, weights), _trunk(x_sample, consts, weights))
```

```python
import functools

import numpy as np
import jax
import jax.numpy as jnp
from jax import lax
from jax.experimental import pallas as pl
from jax.experimental.pallas import tpu as pltpu

F32 = jnp.float32
BF16 = jnp.bfloat16

D_MODEL = 2048
SEQ = 2048
DEPTH = 4
CONV_CH = 1024
N_HEADS = 8
HEAD_DIM = 128
ATTN_W = N_HEADS * HEAD_DIM
DIL_PAIRS = ((128, 1), (512, 4), (2048, 16))
ROPE_THETA = 10000.0
N_FOURIER_GROUPS = 4
FOURIER_CH = D_MODEL // N_FOURIER_GROUPS
D_FF = 5632
ALPHA = (2 * DEPTH) ** 0.25
LN_EPS = 1e-5

MIB = 1024 * 1024
NEG_BIG = -1e30

MM_TM = 1024
SUB_TM = 256
LN_TM = 512
ATT_QB = 256
GATE_TC = 256
FFN_TF = 512
FFN_RB = 512
FFN_RC = 256
FFN_PAD = 8


def _params(sem, vmem_mib):
    return pltpu.CompilerParams(dimension_semantics=sem, vmem_limit_bytes=vmem_mib * MIB)


def _const_spec(shape):
    nd = len(shape)
    return pl.BlockSpec(shape, lambda *_: (0,) * nd, pipeline_mode=pl.Buffered(1))


def _layer_spec(layer, shape):
    nd = len(shape)
    return pl.BlockSpec((None,) + tuple(shape), lambda *_: (layer,) + (0,) * nd,
                        pipeline_mode=pl.Buffered(1))


def _layer_norm(z, g, b):
    mu = jnp.mean(z, axis=-1, keepdims=True)
    zc = z - mu
    var = jnp.mean(zc * zc, axis=-1, keepdims=True)
    return zc * lax.rsqrt(var + LN_EPS) * g + b


def _shift_rows(t):
    n = t.shape[0]
    row = lax.broadcasted_iota(jnp.int32, t.shape, 0)
    prev = jnp.where(row == 0, 0.0, pltpu.roll(t, 1, axis=0))
    nxt = jnp.where(row == n - 1, 0.0, pltpu.roll(t, n - 1, axis=0))
    return prev, nxt


def _conv_branch_kernel(x_ref, wb_ref, wc_ref, wx_ref, cw_ref, o_ref):
    x = x_ref[0]
    cg = jnp.dot(x, wc_ref[...], preferred_element_type=F32)
    xv = jnp.dot(x, wx_ref[...], preferred_element_type=F32)
    t = cg * xv
    prev, nxt = _shift_rows(t)
    w = cw_ref[...]
    y = prev * w[0:1] + t * w[1:2] + nxt * w[2:3]
    bg = jnp.dot(x, wb_ref[...], preferred_element_type=F32)
    o_ref[0] = (bg * y).astype(BF16)


def _conv_branch(xb, w_in, conv_w, layer):
    bsz = xb.shape[0]
    nb = CONV_CH // GATE_TC
    return pl.pallas_call(
        _conv_branch_kernel,
        grid=(bsz, nb),
        in_specs=[pl.BlockSpec((1, SEQ, D_MODEL), lambda i, j: (i, 0, 0)),
                  pl.BlockSpec((None, D_MODEL, GATE_TC), lambda i, j: (layer, 0, j)),
                  pl.BlockSpec((None, D_MODEL, GATE_TC), lambda i, j: (layer, 0, nb + j)),
                  pl.BlockSpec((None, D_MODEL, GATE_TC), lambda i, j: (layer, 0, 2 * nb + j)),
                  pl.BlockSpec((None, 3, GATE_TC), lambda i, j: (layer, 0, j))],
        out_specs=pl.BlockSpec((1, SEQ, GATE_TC), lambda i, j: (i, 0, j)),
        out_shape=jax.ShapeDtypeStruct((bsz, SEQ, CONV_CH), BF16),
        compiler_params=_params(("parallel", "arbitrary"), 48),
        name="conv_branch",
    )(xb, w_in, w_in, w_in, conv_w)


def _qkv_kernel(x_ref, w_ref, cos_ref, sin_ref, o_ref):
    j = pl.program_id(1)

    @pl.when(j == 2)
    def _():
        acc = jnp.dot(x_ref[...], w_ref[...], preferred_element_type=F32)
        for h in range(N_HEADS):
            o_ref[h] = acc[:, h * HEAD_DIM:(h + 1) * HEAD_DIM].astype(BF16)

    @pl.when(j < 2)
    def _():
        scale = jnp.where(j == 0, HEAD_DIM ** -0.5, 1.0).astype(F32)
        for s in range(MM_TM // SUB_TM):
            rows = slice(s * SUB_TM, (s + 1) * SUB_TM)
            acc = jnp.dot(x_ref[rows, :], w_ref[...], preferred_element_type=F32)
            cos = cos_ref[rows, :]
            sin = sin_ref[rows, :]
            for h in range(N_HEADS):
                xh = acc[:, h * HEAD_DIM:(h + 1) * HEAD_DIM]
                rot = pltpu.roll(xh, HEAD_DIM // 2, axis=1)
                o_ref[h, rows, :] = ((xh * cos + rot * sin) * scale).astype(BF16)


def _project_qkv(xb, w_in, layer, cos_full, sin_signed):
    m, k = xb.shape
    c0 = 3 * CONV_CH // ATTN_W
    n_pos_tiles = SEQ // MM_TM
    return pl.pallas_call(
        _qkv_kernel,
        grid=(m // MM_TM, 3),
        in_specs=[pl.BlockSpec((MM_TM, k), lambda i, j: (i, 0)),
                  pl.BlockSpec((None, k, ATTN_W), lambda i, j: (layer, 0, c0 + j)),
                  pl.BlockSpec((MM_TM, HEAD_DIM), lambda i, j: (i % n_pos_tiles, 0)),
                  pl.BlockSpec((MM_TM, HEAD_DIM), lambda i, j: (i % n_pos_tiles, 0))],
        out_specs=pl.BlockSpec((N_HEADS, MM_TM, HEAD_DIM), lambda i, j: (j, i, 0)),
        out_shape=jax.ShapeDtypeStruct((3 * N_HEADS, m, HEAD_DIM), BF16),
        compiler_params=_params(("parallel", "arbitrary"), 40),
        name="project_qkv",
    )(xb, w_in, cos_full, sin_signed)


WIDE_WINDOW, WIDE_DIL = DIL_PAIRS[-1]
NARROW_PAIRS = DIL_PAIRS[:-1]
PHASE_LEN = SEQ // WIDE_DIL
GATHER_STRIDE = 4
QUARTER = SEQ // GATHER_STRIDE


def _band_bias_table():
    a = np.arange(ATT_QB)[:, None]
    c = np.arange(3 * ATT_QB)[None, :]
    d = c - ATT_QB - a
    count = np.zeros(d.shape, np.int64)
    for window, dilation in NARROW_PAIRS:
        count += ((d % dilation == 0) & (np.abs(d) <= window // 2)).astype(np.int64)
    bias = np.where(count > 0, np.log(np.maximum(count, 1).astype(np.float64)), NEG_BIG)
    return bias.astype(np.float32)


def _phase_bias_table():
    l = np.arange(PHASE_LEN)
    ok = np.abs(l[None, :] - l[:, None]) <= WIDE_WINDOW // (2 * WIDE_DIL)
    return np.where(ok, 0.0, NEG_BIG).astype(np.float32)


def _attn_kernel(q_ref, k_ref, v_ref, band_bias_ref, phase_bias_ref, o_ref,
                 nat_ref, gq_ref, gk_ref, gv_ref, go_ref, gl_ref, wide_o_ref, wide_lse_ref):
    contract_last = (((1,), (1,)), ((), ()))

    for src_ref, dst_ref in ((q_ref, gq_ref), (k_ref, gk_ref), (v_ref, gv_ref)):
        nat_ref[...] = src_ref[...].astype(F32)
        for p4 in range(GATHER_STRIDE):
            dst_ref[p4 * QUARTER:(p4 + 1) * QUARTER, :] = nat_ref[pl.ds(p4, QUARTER, stride=GATHER_STRIDE), :]

    phase_rows = [pl.ds(p4 * QUARTER + a, PHASE_LEN, stride=GATHER_STRIDE)
                  for p4 in range(GATHER_STRIDE) for a in range(GATHER_STRIDE)]
    qkv_phases = [(gq_ref[r, :].astype(BF16), gk_ref[r, :].astype(BF16), gv_ref[r, :].astype(BF16))
                  for r in phase_rows]
    phase_bias = phase_bias_ref[...]
    scores = [lax.dot_general(qp, kp, contract_last, preferred_element_type=F32) + phase_bias
              for qp, kp, _ in qkv_phases]
    probs = []
    for s in scores:
        m = jnp.max(s, axis=-1, keepdims=True)
        e = jnp.exp(s - m)
        den = jnp.sum(e, axis=-1, keepdims=True)
        probs.append((e.astype(BF16), den, m + jnp.log(den)))
    for r, (e, den, lse), (_, _, vp) in zip(phase_rows, probs, qkv_phases):
        go_ref[r, :] = jnp.dot(e, vp, preferred_element_type=F32) / den
        gl_ref[r, :] = jnp.broadcast_to(lse, (PHASE_LEN, HEAD_DIM))
    for p4 in range(GATHER_STRIDE):
        nat_rows = pl.ds(p4, QUARTER, stride=GATHER_STRIDE)
        wide_o_ref[nat_rows, :] = go_ref[p4 * QUARTER:(p4 + 1) * QUARTER, :]
        wide_lse_ref[nat_rows, :] = gl_ref[p4 * QUARTER:(p4 + 1) * QUARTER, :]

    nb = SEQ // ATT_QB

    def key_rows(qi):
        return slice(max(0, qi - 1) * ATT_QB, min(nb, qi + 2) * ATT_QB)

    def band_scores(qi):
        k0, k1 = max(0, qi - 1), min(nb, qi + 2)
        s = lax.dot_general(q_ref[qi * ATT_QB:(qi + 1) * ATT_QB, :], k_ref[key_rows(qi), :],
                            contract_last, preferred_element_type=F32)
        b0 = (k0 - qi + 1) * ATT_QB
        return s + band_bias_ref[:, b0:b0 + (k1 - k0) * ATT_QB]

    s_next = band_scores(0)
    for qi in range(nb):
        s = s_next
        if qi + 1 < nb:
            s_next = band_scores(qi + 1)
        rows = slice(qi * ATT_QB, (qi + 1) * ATT_QB)
        wide_lse = wide_lse_ref[rows, 0:1]
        m = jnp.maximum(jnp.max(s, axis=-1, keepdims=True), wide_lse)
        p = jnp.exp(s - m)
        wide_w = jnp.exp(wide_lse - m)
        den = jnp.sum(p, axis=-1, keepdims=True) + wide_w
        o = jnp.dot(p.astype(BF16), v_ref[key_rows(qi), :], preferred_element_type=F32)
        o_ref[0, rows, :] = ((o + wide_o_ref[rows, :] * wide_w) / den).astype(BF16)


def _attention(qkv, band_bias, phase_bias):
    b = qkv.shape[1]
    head_blk = (None, None, SEQ, HEAD_DIM)
    return pl.pallas_call(
        _attn_kernel,
        grid=(b, N_HEADS),
        in_specs=[pl.BlockSpec(head_blk, lambda i, h: (h, i, 0, 0)),
                  pl.BlockSpec(head_blk, lambda i, h: (N_HEADS + h, i, 0, 0)),
                  pl.BlockSpec(head_blk, lambda i, h: (2 * N_HEADS + h, i, 0, 0)),
                  _const_spec(band_bias.shape), _const_spec(phase_bias.shape)],
        out_specs=pl.BlockSpec((1, SEQ, HEAD_DIM), lambda i, h: (i, 0, h)),
        out_shape=jax.ShapeDtypeStruct((b, SEQ, ATTN_W), BF16),
        scratch_shapes=[pltpu.VMEM((SEQ, HEAD_DIM), F32)] * 8,
        compiler_params=_params(("parallel", "arbitrary"), 40),
        name="dilated_attention",
    )(qkv, qkv, qkv, band_bias, phase_bias)


HALF_SEQ = SEQ // 2
MID_ROWS = 16


def _dft_matrices(n, scale):
    idx = jnp.arange(n, dtype=jnp.int32)
    m = (idx[:, None] * idx[None, :]) % n
    ang = m.astype(F32) * (2.0 * np.pi / n)
    return jnp.cos(ang) * scale, jnp.sin(ang) * scale


def _row_reversal_matrix():
    j = np.arange(HALF_SEQ)
    r = np.zeros((HALF_SEQ, HALF_SEQ), np.float32)
    r[j, (HALF_SEQ - j) % HALF_SEQ] = 1.0
    return r


def _fourier_kernel(x_ref, wc_ref, ch_ref, sh_ref, cmid_ref, rev_ref, y_ref):
    x = x_ref[0]
    ab = jnp.dot(x, wc_ref[...], preferred_element_type=F32)
    a = ab[:, :FOURIER_CH].astype(BF16)
    b = ab[:, FOURIER_CH:].astype(BF16)
    p = jnp.dot(ch_ref[...], a, preferred_element_type=F32)
    q = jnp.dot(sh_ref[...], b, preferred_element_type=F32)
    y_ref[0, :HALF_SEQ, :] = (p - q).astype(BF16)
    mid = jnp.dot(cmid_ref[...], a, preferred_element_type=F32)
    row = lax.broadcasted_iota(jnp.int32, p.shape, 0)
    z = jnp.where(row == 0, mid[0:1, :], p + q).astype(BF16)
    y_ref[0, HALF_SEQ:, :] = jnp.dot(rev_ref[...], z, preferred_element_type=F32).astype(BF16)


def _fourier_mix(xb, consts):
    bsz = xb.shape[0]
    blk = pl.BlockSpec((1, SEQ, FOURIER_CH), lambda i, g: (i, 0, g))
    mats = [consts["dft_ch"], consts["dft_cos_half"], consts["dft_sin_half"],
            consts["dft_cos_mid"], consts["dft_row_rev"]]
    return pl.pallas_call(
        _fourier_kernel,
        grid=(bsz, N_FOURIER_GROUPS),
        in_specs=[blk] + [_const_spec(mat.shape) for mat in mats],
        out_specs=blk,
        out_shape=jax.ShapeDtypeStruct(xb.shape, BF16),
        compiler_params=_params(("parallel", "arbitrary"), 48),
        name="fourier_mix",
    )(xb, *mats)


def _proj_ln_kernel(*refs, n_lhs):
    lhs = refs[:n_lhs]
    ws = refs[n_lhs:2 * n_lhs]
    res_ref, g_ref, b_ref, y_ref, yb_ref = refs[2 * n_lhs:]
    for s in range(LN_TM // SUB_TM):
        rows = slice(s * SUB_TM, (s + 1) * SUB_TM)
        acc = jnp.dot(lhs[0][rows, :], ws[0][...], preferred_element_type=F32)
        for l, w in zip(lhs[1:], ws[1:]):
            acc = acc + jnp.dot(l[rows, :], w[...], preferred_element_type=F32)
        y = _layer_norm(ALPHA * res_ref[rows, :] + acc, g_ref[...], b_ref[...])
        y_ref[rows, :] = y
        yb_ref[rows, :] = y.astype(BF16)


def _project_residual_ln(lhs_list, w, w_layer, res, g, b, ln_layer):
    m, d = res.shape
    n = len(lhs_list)
    in_specs = [pl.BlockSpec((LN_TM, l.shape[1]), lambda i: (i, 0)) for l in lhs_list]
    k0 = 0
    for l in lhs_list:
        kb = l.shape[1]
        in_specs.append(pl.BlockSpec((None, kb, d),
                                     functools.partial(lambda i, blk: (w_layer, blk, 0), blk=k0 // kb),
                                     pipeline_mode=pl.Buffered(1)))
        k0 += kb
    in_specs += [pl.BlockSpec((LN_TM, d), lambda i: (i, 0)),
                 _layer_spec(ln_layer, (1, d)), _layer_spec(ln_layer, (1, d))]
    out_blk = pl.BlockSpec((LN_TM, d), lambda i: (i, 0))
    return pl.pallas_call(
        functools.partial(_proj_ln_kernel, n_lhs=n),
        grid=(m // LN_TM,),
        in_specs=in_specs,
        out_specs=[out_blk, out_blk],
        out_shape=[jax.ShapeDtypeStruct((m, d), F32), jax.ShapeDtypeStruct((m, d), BF16)],
        compiler_params=_params(("parallel",), 48),
        name="project_residual_ln",
    )(*lhs_list, *([w] * n), res, g.reshape(-1, 1, d), b.reshape(-1, 1, d))


def _ffn_kernel(xb_ref, wg_ref, wu_ref, cg_ref, cu_ref, wd_ref, g_ref, b_ref, res_hbm,
                y_hbm, yb_hbm,
                acc_ref, hg_ref, hu_ref, y_buf, yb_buf, res_sem, y_sem, yb_sem, *, nf):
    i = pl.program_id(0)
    j = pl.program_id(1)
    n_seg = SEQ // FFN_RC

    def seg_rows(c):
        return pl.ds(c * FFN_RC, FFN_RC)

    def res_copy(c):
        return pltpu.make_async_copy(res_hbm.at[i, seg_rows(c), :], y_buf.at[c % 3], res_sem.at[c % 3])

    def y_copy(c):
        return pltpu.make_async_copy(y_buf.at[c % 3], y_hbm.at[i, seg_rows(c), :], y_sem.at[c % 3])

    def yb_copy(c):
        return pltpu.make_async_copy(yb_buf.at[c % 2], yb_hbm.at[i, seg_rows(c), :], yb_sem.at[c % 2])

    def up_project(r0, nrows):
        xr = xb_ref[0, r0:r0 + nrows, :]
        hg_ref[FFN_PAD + r0:FFN_PAD + r0 + nrows, :] = jnp.dot(xr, wg_ref[...], preferred_element_type=F32)
        hu_ref[FFN_PAD + r0:FFN_PAD + r0 + nrows, :] = jnp.dot(xr, wu_ref[...], preferred_element_type=F32)

    def down_project(r0, nrows):
        def conv(h_ref, c):
            prev = h_ref[FFN_PAD + r0 - 1:FFN_PAD + r0 - 1 + nrows, :]
            cur = h_ref[FFN_PAD + r0:FFN_PAD + r0 + nrows, :]
            nxt = h_ref[FFN_PAD + r0 + 1:FFN_PAD + r0 + 1 + nrows, :]
            return prev * c[0:1] + cur * c[1:2] + nxt * c[2:3]

        gate = conv(hg_ref, cg_ref[...])
        up = conv(hu_ref, cu_ref[...])
        act = (gate * jax.nn.sigmoid(gate) * up).astype(BF16)
        return jnp.dot(act, wd_ref[...], preferred_element_type=F32)

    @pl.when(jnp.logical_and(i == 0, j == 0))
    def _():
        acc_ref[...] = jnp.zeros_like(acc_ref)
        zpad = jnp.zeros((FFN_PAD, FFN_TF), F32)
        for h_ref in (hg_ref, hu_ref):
            h_ref[0:FFN_PAD, :] = zpad
            h_ref[FFN_PAD + SEQ:, :] = zpad

    @pl.when(j < nf - 1)
    def _():
        for rb in range(SEQ // FFN_RB):
            up_project(rb * FFN_RB, FFN_RB)
        for rb in range(SEQ // FFN_RB):
            r0 = rb * FFN_RB
            acc_ref[r0:r0 + FFN_RB, :] += down_project(r0, FFN_RB)

    @pl.when(j == nf - 1)
    def _():
        up_project(0, FFN_RC)
        up_project(FFN_RC, FFN_RC)
        for s in range(n_seg + 1):
            if s + 2 < n_seg:
                up_project((s + 2) * FFN_RC, FFN_RC)
            if s >= 2:
                y_copy(s - 2).start()
                yb_copy(s - 2).start()
            if s < n_seg:
                if s >= 3:
                    y_copy(s - 3).wait()
                res_copy(s).start()
            if s >= 1:
                res_copy(s - 1).wait()
            if s >= 3:
                yb_copy(s - 3).wait()
            if s >= 1:
                c = s - 1
                rows = slice(c * FFN_RC, (c + 1) * FFN_RC)
                y = _layer_norm(ALPHA * y_buf[c % 3] + acc_ref[rows, :], g_ref[...], b_ref[...])
                y_buf[c % 3] = y
                yb_buf[c % 2] = y.astype(BF16)
                acc_ref[rows, :] = jnp.zeros((FFN_RC, D_MODEL), F32)
            if s < n_seg:
                r0 = s * FFN_RC
                acc_ref[r0:r0 + FFN_RC, :] += down_project(r0, FFN_RC)
        y_copy(n_seg - 1).start()
        yb_copy(n_seg - 1).start()
        for c in (n_seg - 3, n_seg - 2, n_seg - 1):
            y_copy(c).wait()
        for c in (n_seg - 2, n_seg - 1):
            yb_copy(c).wait()


def _ffn(xb, res, w_up, conv_w, w_down, g, b, layer):
    bsz = xb.shape[0]
    d = D_MODEL
    nf = D_FF // FFN_TF
    return pl.pallas_call(
        functools.partial(_ffn_kernel, nf=nf),
        grid=(bsz, nf),
        in_specs=[
            pl.BlockSpec((1, SEQ, d), lambda i, j: (i, 0, 0), pipeline_mode=pl.Buffered(1)),
            pl.BlockSpec((None, d, FFN_TF), lambda i, j: (layer, 0, j)),
            pl.BlockSpec((None, d, FFN_TF), lambda i, j: (layer, 0, nf + j)),
            pl.BlockSpec((None, 3, FFN_TF), lambda i, j: (layer, 0, j)),
            pl.BlockSpec((None, 3, FFN_TF), lambda i, j: (layer, 0, nf + j)),
            pl.BlockSpec((None, FFN_TF, d), lambda i, j: (layer, j, 0)),
            _layer_spec(layer, (1, d)),
            _layer_spec(layer, (1, d)),
            pl.BlockSpec(memory_space=pl.ANY),
        ],
        out_specs=[pl.BlockSpec(memory_space=pl.ANY), pl.BlockSpec(memory_space=pl.ANY)],
        out_shape=[jax.ShapeDtypeStruct((bsz, SEQ, d), F32), jax.ShapeDtypeStruct((bsz, SEQ, d), BF16)],
        scratch_shapes=[pltpu.VMEM((SEQ, d), F32),
                        pltpu.VMEM((SEQ + 2 * FFN_PAD, FFN_TF), F32),
                        pltpu.VMEM((SEQ + 2 * FFN_PAD, FFN_TF), F32),
                        pltpu.VMEM((3, FFN_RC, d), F32),
                        pltpu.VMEM((2, FFN_RC, d), BF16),
                        pltpu.SemaphoreType.DMA((3,)),
                        pltpu.SemaphoreType.DMA((3,)),
                        pltpu.SemaphoreType.DMA((2,))],
        compiler_params=_params(("arbitrary", "arbitrary"), 58),
        name="conv_ffn_ln",
    )(xb, w_up, w_up, conv_w, conv_w, w_down, g.reshape(-1, 1, d), b.reshape(-1, 1, d), res)


def _rope_tables():
    half = HEAD_DIM // 2
    inv = 1.0 / (ROPE_THETA ** (jnp.arange(half, dtype=F32) / half))
    ang = jnp.arange(SEQ, dtype=F32)[:, None] * inv[None, :]
    cos = jnp.cos(ang)
    sin = jnp.sin(ang)
    return jnp.concatenate([cos, cos], axis=-1), jnp.concatenate([-sin, sin], axis=-1)


def _trunk(x, consts, weights):
    bsz = x.shape[0]
    m = bsz * SEQ
    d = D_MODEL
    xb = x.astype(BF16)
    for l in range(DEPTH):
        i = l // 2
        x2 = x.reshape(m, d)
        if l % 2 == 0:
            w_in = weights["w_in_mix"]
            y_conv = _conv_branch(xb, w_in, weights["conv_short"], i)
            qkv = _project_qkv(xb.reshape(m, d), w_in, i, consts["rope_cos"], consts["rope_sin"])
            y_attn = _attention(qkv.reshape(3 * N_HEADS, bsz, SEQ, HEAD_DIM),
                                consts["band_bias"], consts["phase_bias"])
            lhs = [y_conv.reshape(m, CONV_CH), y_attn.reshape(m, ATTN_W)]
            w_out = weights["w_out_mix"]
        else:
            lhs = [_fourier_mix(xb, consts).reshape(m, d)]
            w_out = weights["w_out_fourier"]
        x2, xb2 = _project_residual_ln(lhs, w_out, i, x2, weights["ln_mix_g"], weights["ln_mix_b"], l)
        x, xb = _ffn(xb2.reshape(bsz, SEQ, d), x2.reshape(bsz, SEQ, d), weights["w_up"],
                     weights["conv_ffn_w"], weights["w_down"], weights["ln_ffn_g"], weights["ln_ffn_b"], l)
    return x


def kernel(x_prompt, x_sample, w_in_mix, conv_short, w_out_mix, w_out_fourier, ln_mix_g, ln_mix_b,
           w_up, conv_ffn_w, w_down, ln_ffn_g, ln_ffn_b):
    weights = {
        "w_in_mix": w_in_mix.astype(BF16), "conv_short": conv_short,
        "w_out_mix": w_out_mix.astype(BF16), "w_out_fourier": w_out_fourier.astype(BF16),
        "ln_mix_g": ln_mix_g, "ln_mix_b": ln_mix_b,
        "w_up": w_up.astype(BF16), "conv_ffn_w": conv_ffn_w, "w_down": w_down.astype(BF16),
        "ln_ffn_g": ln_ffn_g, "ln_ffn_b": ln_ffn_b,
    }
    cos_c, sin_c = _dft_matrices(FOURIER_CH, 2.0 ** -5)
    cos_s, sin_s = _dft_matrices(SEQ, 2.0 ** -5)
    rope_cos, rope_sin = _rope_tables()
    cos_mid = jnp.zeros((MID_ROWS, SEQ), F32).at[0].set(cos_s[HALF_SEQ])
    consts = {
        "rope_cos": rope_cos, "rope_sin": rope_sin,
        "band_bias": jnp.asarray(_band_bias_table()), "phase_bias": jnp.asarray(_phase_bias_table()),
        "dft_ch": jnp.concatenate([cos_c, sin_c], axis=1).astype(BF16),
        "dft_cos_half": cos_s[:HALF_SEQ].astype(BF16), "dft_sin_half": sin_s[:HALF_SEQ].astype(BF16),
        "dft_cos_mid": cos_mid.astype(BF16),
        "dft_row_rev": jnp.asarray(_row_reversal_matrix()).astype(BF16),
    }
    return (_trunk(x_prompt, consts, weights), _trunk(x_sample, consts, weights))
```
